```python
import jax, jax.numpy as jnp
from jax import lax
import numpy as np

D_MODEL = 1024
BATCH = 32
SEQ = 2048
DEPTH = 4

GRID_W = 64
CTX_LEN = 256
HEAD_DIM = 64
N_Q_HEADS = 8
N_KV_HEADS = 2
Q_GROUP = N_Q_HEADS // N_KV_HEADS
ATTN_WIDTH = N_Q_HEADS * HEAD_DIM
KV_WIDTH = N_KV_HEADS * HEAD_DIM
AXIS_DIM = HEAD_DIM // 2
ROPE_THETA = 10000.0
Q_BLOCK = 128
ATTN_SCALE = HEAD_DIM ** -0.5
CONF_WIDTH = D_MODEL // 2
CONF_KERNEL = 31
SC_WIDTH = D_MODEL // 2
SC_KERNEL = 3
N_BRANCHES = 3
N_MOD = 6
FFN_HIDDEN = -(-8 * D_MODEL // (3 * 256)) * 256
EPS = 1e-6

OFF_K = ATTN_WIDTH
OFF_V = OFF_K + KV_WIDTH
OFF_CONF = OFF_V + KV_WIDTH
OFF_SC = OFF_CONF + 2 * CONF_WIDTH
OFF_GATE = OFF_SC + 3 * SC_WIDTH
IN_WIDTH = OFF_GATE + N_BRANCHES * D_MODEL

kernel_name = 'hybrid_gqa_conformer_shortconv_dit_block'


def rms_norm(x):
    xf = x.astype(jnp.float32)
    return (xf * lax.rsqrt(jnp.mean(xf * xf, axis=-1, keepdims=True) + EPS)).astype(x.dtype)


def head_rms_norm(x, g):
    return rms_norm(x) * g


def layer_norm(x, g, b):
    xf = x.astype(jnp.float32)
    mu = jnp.mean(xf, axis=-1, keepdims=True)
    var = jnp.mean(jnp.square(xf - mu), axis=-1, keepdims=True)
    return ((xf - mu) * lax.rsqrt(var + EPS)).astype(x.dtype) * g + b


def rope_tables(seq_len):
    rows = seq_len // GRID_W
    r_ids, c_ids = jnp.meshgrid(jnp.arange(rows), jnp.arange(GRID_W), indexing='ij')
    r_ids = r_ids.reshape(-1).astype(jnp.float32)
    c_ids = c_ids.reshape(-1).astype(jnp.float32)
    freqs = ROPE_THETA ** (-jnp.arange(0, AXIS_DIM, 2, dtype=jnp.float32) / AXIS_DIM)
    ang_r = r_ids[:, None] * freqs
    ang_c = c_ids[:, None] * freqs
    return (jnp.cos(ang_r)[:, None, :], jnp.sin(ang_r)[:, None, :],
            jnp.cos(ang_c)[:, None, :], jnp.sin(ang_c)[:, None, :])


def _rotate_half(xp, cos, sin):
    x1, x2 = jnp.split(xp, 2, axis=-1)
    return jnp.concatenate([x1 * cos - x2 * sin, x1 * sin + x2 * cos], axis=-1)


def apply_rope_2d(x, tabs):
    cos_r, sin_r, cos_c, sin_c = tabs
    xf = x.astype(jnp.float32)
    out = jnp.concatenate([_rotate_half(xf[..., :AXIS_DIM], cos_r, sin_r),
                           _rotate_half(xf[..., AXIS_DIM:], cos_c, sin_c)], axis=-1)
    return out.astype(x.dtype)


def depthwise_conv(x, w):
    k = w.shape[0]
    return lax.conv_general_dilated(x, w[:, None, :].astype(x.dtype), window_strides=(1,),
                                    padding=[(k // 2, k // 2)],
                                    dimension_numbers=('NWC', 'WIO', 'NWC'),
                                    feature_group_count=x.shape[-1])


def latent_attention(q, k_all, v_all):
    b, s = q.shape[0], q.shape[1]
    n_blk = s // Q_BLOCK
    qb = q.reshape(b, n_blk, Q_BLOCK, N_KV_HEADS, Q_GROUP, HEAD_DIM).transpose(1, 0, 2, 3, 4, 5)

    def one_block(qi):
        sc = jnp.einsum('bqhgd,bkhd->bhgqk', qi, k_all).astype(jnp.float32) * ATTN_SCALE
        p = jax.nn.softmax(sc, axis=-1).astype(v_all.dtype)
        return jnp.einsum('bhgqk,bkhd->bqhgd', p, v_all)

    o = lax.map(one_block, qb)
    return o.transpose(1, 0, 2, 3, 4, 5).reshape(b, s, ATTN_WIDTH)


def context_attention(qc, kc, vc):
    b, l = qc.shape[0], qc.shape[1]
    qg = qc.reshape(b, l, N_KV_HEADS, Q_GROUP, HEAD_DIM)
    sc = jnp.einsum('bqhgd,bkhd->bhgqk', qg, kc).astype(jnp.float32) * ATTN_SCALE
    p = jax.nn.softmax(sc, axis=-1).astype(vc.dtype)
    return jnp.einsum('bhgqk,bkhd->bqhgd', p, vc).reshape(b, l, ATTN_WIDTH)


def conformer_branch(u, dw_w, dw_b, ln_g, ln_b, w_out):
    a, g = jnp.split(u, 2, axis=-1)
    h = a * jax.nn.sigmoid(g)
    h = depthwise_conv(h, dw_w) + dw_b
    h = jax.nn.silu(layer_norm(h, ln_g, ln_b))
    return h @ w_out


def shortconv_branch(u, dw_w, w_out):
    bg, cg, xs = jnp.split(u, 3, axis=-1)
    return (bg * depthwise_conv(cg * xs, dw_w)) @ w_out


def merge_branches(attn_heads, conf_u, sc_u, gate_logits, w_attn_o, conf_dw_w, conf_dw_b,
                   conf_ln_g, conf_ln_b, w_conf_out, sc_dw_w, w_sc_out, w_mix_out):
    y_attn = attn_heads @ w_attn_o
    y_conf = conformer_branch(conf_u, conf_dw_w, conf_dw_b, conf_ln_g, conf_ln_b, w_conf_out)
    y_sc = shortconv_branch(sc_u, sc_dw_w, w_sc_out)
    g = jax.nn.sigmoid(gate_logits.reshape(gate_logits.shape[:-1] + (N_BRANCHES, D_MODEL)))
    merged = g[..., 0, :] * y_attn + g[..., 1, :] * y_conf + g[..., 2, :] * y_sc
    return merged @ w_mix_out


def swiglu(h, w_in, w_out):
    a, b = jnp.split(h @ w_in, 2, axis=-1)
    return (jax.nn.silu(a) * b) @ w_out


def setup_inputs(seed: int = 0) -> dict:
    key = jax.random.key(seed)
    ks = jax.random.split(key, 24)
    f32 = jnp.float32
    L, D = DEPTH, D_MODEL

    def nrm(k, shape, scale):
        return jax.random.normal(k, shape, f32) * scale

    return {
        'x': nrm(ks[0], (BATCH, SEQ, D), 1.0),
        'c': nrm(ks[1], (BATCH, D), 1.0),
        'ctx': nrm(ks[2], (BATCH, CTX_LEN, D), 1.0),
        'c_ctx': nrm(ks[3], (D,), 1.0),
        'w_ada': nrm(ks[4], (L, D, N_MOD * D), 0.5 * D ** -0.5),
        'b_ada': nrm(ks[5], (L, N_MOD * D), 0.02),
        'w_in': nrm(ks[6], (L, D, IN_WIDTH), D ** -0.5),
        'q_norm': 1.0 + nrm(ks[7], (L, HEAD_DIM), 0.02),
        'k_norm': 1.0 + nrm(ks[8], (L, HEAD_DIM), 0.02),
        'w_attn_o': nrm(ks[9], (L, ATTN_WIDTH, D), ATTN_WIDTH ** -0.5),
        'conf_dw_w': nrm(ks[10], (L, CONF_KERNEL, CONF_WIDTH), CONF_KERNEL ** -0.5),
        'conf_dw_b': nrm(ks[11], (L, CONF_WIDTH), 0.02),
        'conf_ln_g': 1.0 + nrm(ks[12], (L, CONF_WIDTH), 0.02),
        'conf_ln_b': nrm(ks[13], (L, CONF_WIDTH), 0.02),
        'w_conf_out': nrm(ks[14], (L, CONF_WIDTH, D), CONF_WIDTH ** -0.5),
        'sc_dw_w': nrm(ks[15], (L, SC_KERNEL, SC_WIDTH), SC_KERNEL ** -0.5),
        'w_sc_out': nrm(ks[16], (L, SC_WIDTH, D), SC_WIDTH ** -0.5),
        'w_mix_out': nrm(ks[17], (L, D, D), D ** -0.5),
        'w_ffn_in': nrm(ks[18], (L, D, 2 * FFN_HIDDEN), D ** -0.5),
        'w_ffn_out': nrm(ks[19], (L, FFN_HIDDEN, D), FFN_HIDDEN ** -0.5),
    }


def reference(x, c, ctx, c_ctx, w_ada, b_ada, w_in, q_norm, k_norm, w_attn_o, conf_dw_w, conf_dw_b,
              conf_ln_g, conf_ln_b, w_conf_out, sc_dw_w, w_sc_out, w_mix_out, w_ffn_in, w_ffn_out):
    b, s = x.shape[0], x.shape[1]
    lc = ctx.shape[1]
    rope = rope_tables(s)
    for i in range(DEPTH):
        last = i == DEPTH - 1
        wi = w_in[i]
        branch_w = (w_attn_o[i], conf_dw_w[i], conf_dw_b[i], conf_ln_g[i], conf_ln_b[i],
                    w_conf_out[i], sc_dw_w[i], w_sc_out[i], w_mix_out[i])
        mod = (jax.nn.silu(c) @ w_ada[i] + b_ada[i]).reshape(b, N_MOD, 1, D_MODEL)
        modc = (jax.nn.silu(c_ctx) @ w_ada[i] + b_ada[i]).reshape(N_MOD, D_MODEL)

        hc = rms_norm(ctx) * (1 + modc[1]) + modc[0]
        kvc = hc @ wi[:, OFF_K:OFF_CONF]
        kc = head_rms_norm(kvc[..., :KV_WIDTH].reshape(b, lc, N_KV_HEADS, HEAD_DIM), k_norm[i])
        vc = kvc[..., KV_WIDTH:].reshape(b, lc, N_KV_HEADS, HEAD_DIM)

        h = rms_norm(x) * (1 + mod[:, 1]) + mod[:, 0]
        p = h @ wi
        q = apply_rope_2d(head_rms_norm(p[..., :OFF_K].reshape(b, s, N_Q_HEADS, HEAD_DIM), q_norm[i]), rope)
        k = apply_rope_2d(head_rms_norm(p[..., OFF_K:OFF_V].reshape(b, s, N_KV_HEADS, HEAD_DIM), k_norm[i]), rope)
        v = p[..., OFF_V:OFF_CONF].reshape(b, s, N_KV_HEADS, HEAD_DIM)
        attn = latent_attention(q, jnp.concatenate([kc, k], axis=1), jnp.concatenate([vc, v], axis=1))
        mixed = merge_branches(attn, p[..., OFF_CONF:OFF_SC], p[..., OFF_SC:OFF_GATE], p[..., OFF_GATE:], *branch_w)
        x_new = x + mod[:, 2] * mixed
        h2 = rms_norm(x_new) * (1 + mod[:, 4]) + mod[:, 3]
        x_new = x_new + mod[:, 5] * swiglu(h2, w_ffn_in[i], w_ffn_out[i])

        if not last:
            qc = head_rms_norm((hc @ wi[:, :OFF_K]).reshape(b, lc, N_Q_HEADS, HEAD_DIM), q_norm[i])
            pc = hc @ wi[:, OFF_CONF:]
            attn_c = context_attention(qc, kc, vc)
            mixed_c = merge_branches(attn_c, pc[..., :OFF_SC - OFF_CONF], pc[..., OFF_SC - OFF_CONF:OFF_GATE - OFF_CONF],
                                     pc[..., OFF_GATE - OFF_CONF:], *branch_w)
            ctx = ctx + modc[2] * mixed_c
            h2c = rms_norm(ctx) * (1 + modc[4]) + modc[3]
            ctx = ctx + modc[5] * swiglu(h2c, w_ffn_in[i], w_ffn_out[i])
        x = x_new
    return x
```

```python
import functools

import jax
import jax.numpy as jnp
from jax import lax
from jax.experimental import pallas as pl
from jax.experimental.pallas import tpu as pltpu

F32 = jnp.float32
BF16 = jnp.bfloat16

D_MODEL = 1024
HEAD_DIM = 64
N_Q_HEADS = 8
N_KV_HEADS = 2
Q_GROUP = N_Q_HEADS // N_KV_HEADS
ATTN_WIDTH = N_Q_HEADS * HEAD_DIM
KV_WIDTH = N_KV_HEADS * HEAD_DIM
AXIS_DIM = HEAD_DIM // 2
ROPE_THETA = 10000.0
GRID_W = 64
ATTN_SCALE = HEAD_DIM ** -0.5
CONF_WIDTH = D_MODEL // 2
CONF_KERNEL = 31
SC_WIDTH = D_MODEL // 2
SC_KERNEL = 3
N_BRANCHES = 3
N_MOD = 6
FFN_HIDDEN = 2816
EPS = 1e-6

OFF_K = ATTN_WIDTH
OFF_V = OFF_K + KV_WIDTH
OFF_CONF = OFF_V + KV_WIDTH
OFF_SC = OFF_CONF + 2 * CONF_WIDTH
OFF_GATE = OFF_SC + 3 * SC_WIDTH
IN_WIDTH = OFF_GATE + N_BRANCHES * D_MODEL

LANES = 128
QK_WIDTH = ATTN_WIDTH + KV_WIDTH
REST_WIDTH = IN_WIDTH - QK_WIDTH
R_V = 0
R_CONF_A = R_V + KV_WIDTH
R_CONF_G = R_CONF_A + CONF_WIDTH
R_SC_B = R_CONF_G + CONF_WIDTH
R_SC_C = R_SC_B + SC_WIDTH
R_SC_X = R_SC_C + SC_WIDTH
R_GATE = R_SC_X + SC_WIDTH
HALO = 16
VMEM_LIMIT = 56 * 1024 * 1024


def _cparams(sem):
    return pltpu.CompilerParams(dimension_semantics=sem, vmem_limit_bytes=VMEM_LIMIT)


def _sigmoid(x):
    return 1.0 / (1.0 + jnp.exp(-x))


def _mod_kernel(c_ref, w_ref, b_ref, o_ref):
    a = c_ref[...]
    a = (a * _sigmoid(a)).astype(BF16)
    o_ref[...] = jnp.dot(a, w_ref[...].astype(BF16), preferred_element_type=F32) + b_ref[...]


def _modulation(cc, w_ada, b_ada):
    n_layers, d, n = w_ada.shape
    mp = cc.shape[0]
    tn = 1536
    return pl.pallas_call(
        _mod_kernel,
        grid=(n_layers, n // tn),
        in_specs=[
            pl.BlockSpec((mp, d), lambda l, j: (0, 0)),
            pl.BlockSpec((None, d, tn), lambda l, j: (l, 0, j)),
            pl.BlockSpec((None, 1, tn), lambda l, j: (l, 0, j)),
        ],
        out_specs=pl.BlockSpec((None, mp, tn), lambda l, j: (l, 0, j)),
        out_shape=jax.ShapeDtypeStruct((n_layers, mp, n), F32),
        compiler_params=_cparams(("arbitrary", "arbitrary")),
        name="adaln_mod",
    )(cc, w_ada, b_ada.reshape(n_layers, 1, n))


def _mod_rows(i, tm, lc, modb_ref, modc_ref, idx):
    row = i * tm + lax.broadcasted_iota(jnp.int32, (tm, 1), 0)
    is_ctx = row < lc
    return [jnp.where(is_ctx, modc_ref[k:k + 1, :], modb_ref[k:k + 1, :]) for k in idx]


def _modulated_rms(x, shift, scale):
    ms = jnp.mean(x * x, axis=-1, keepdims=True)
    return (x * lax.rsqrt(ms + EPS)) * (1.0 + scale) + shift


def _inproj_kernel(x_ref, modb_ref, modc_ref, w_ref, g_ref, cos_ref, sin_ref, seg_ref,
                   oqk_ref, orest_ref, h_scr, *, tm, lc):
    i = pl.program_id(1)
    j = pl.program_id(2)

    @pl.when(j == 0)
    def _():
        shift, scale = _mod_rows(i, tm, lc, modb_ref, modc_ref, (0, 1))
        h_scr[...] = _modulated_rms(x_ref[...], shift, scale).astype(BF16)

    acc = jnp.dot(h_scr[...], w_ref[...], preferred_element_type=F32)

    @pl.when(j == 0)
    def _():
        ms = jnp.dot((acc * acc).astype(BF16), seg_ref[...], preferred_element_type=F32)
        y = acc * lax.rsqrt(ms + EPS) * g_ref[...]
        cos = cos_ref[...]
        sin = sin_ref[...]
        lane = lax.broadcasted_iota(jnp.int32, (tm, LANES), 1)
        first = (lane % (2 * (AXIS_DIM // 2))) < (AXIS_DIM // 2)
        for s in range(QK_WIDTH // LANES):
            ys = y[:, s * LANES:(s + 1) * LANES]
            partner = jnp.where(first, pltpu.roll(ys, LANES - AXIS_DIM // 2, 1), pltpu.roll(ys, AXIS_DIM // 2, 1))
            oqk_ref[:, s * LANES:(s + 1) * LANES] = (ys * cos + partner * sin).astype(BF16)

    @pl.when(j > 0)
    def _():
        orest_ref[...] = acc.astype(BF16)


def _in_proj(xs, mod_l, w, gvec, cos_t, sin_t, seg, *, lc, tm):
    b, t, d = xs.shape
    n_tiles = IN_WIDTH // QK_WIDTH
    nb = mod_l.shape[0] - 1
    kern = functools.partial(_inproj_kernel, tm=tm, lc=lc)
    return pl.pallas_call(
        kern,
        grid=(b, t // tm, n_tiles),
        in_specs=[
            pl.BlockSpec((None, tm, d), lambda bi, i, j: (bi, i, 0)),
            pl.BlockSpec((None, N_MOD, d), lambda bi, i, j: (bi, 0, 0)),
            pl.BlockSpec((None, N_MOD, d), lambda bi, i, j: (nb, 0, 0)),
            pl.BlockSpec((d, QK_WIDTH), lambda bi, i, j: (0, j)),
            pl.BlockSpec((1, QK_WIDTH), lambda bi, i, j: (0, 0)),
            pl.BlockSpec((tm, LANES), lambda bi, i, j: (i, 0)),
            pl.BlockSpec((tm, LANES), lambda bi, i, j: (i, 0)),
            pl.BlockSpec((QK_WIDTH, QK_WIDTH), lambda bi, i, j: (0, 0)),
        ],
        out_specs=[
            pl.BlockSpec((None, tm, QK_WIDTH), lambda bi, i, j: (bi, i, 0)),
            pl.BlockSpec((None, tm, QK_WIDTH), lambda bi, i, j: (bi, i, jnp.maximum(j - 1, 0))),
        ],
        out_shape=[
            jax.ShapeDtypeStruct((b, t, QK_WIDTH), BF16),
            jax.ShapeDtypeStruct((b, t, REST_WIDTH), BF16),
        ],
        scratch_shapes=[pltpu.VMEM((tm, d), BF16)],
        compiler_params=_cparams(("arbitrary", "arbitrary", "arbitrary")),
        name="in_proj",
    )(xs, mod_l, mod_l, w, gvec, cos_t, sin_t, seg)


def _attn_kernel(q_ref, k_ref, v_ref, o_ref, *, tq, lc, t):
    i = pl.program_id(1)

    def run(nk):
        k = k_ref[0:nk, :]
        v = v_ref[0:nk, :]
        lane = lax.broadcasted_iota(jnp.int32, (tq, LANES), 1)
        lo = lane < HEAD_DIM
        for p in range(ATTN_WIDTH // LANES):
            qt = q_ref[:, p * LANES:(p + 1) * LANES]
            outs = []
            for mask in (lo, jnp.logical_not(lo)):
                qm = jnp.where(mask, qt, jnp.zeros_like(qt))
                s = lax.dot_general(qm, k, (((1,), (1,)), ((), ())), preferred_element_type=F32)
                m = jnp.max(s, axis=-1, keepdims=True)
                e = jnp.exp(s - m)
                l = jnp.sum(e, axis=-1, keepdims=True)
                o = jnp.dot(e.astype(BF16), v, preferred_element_type=F32)
                outs.append(o / l)
            o_ref[:, p * LANES:(p + 1) * LANES] = jnp.where(lo, outs[0], outs[1]).astype(BF16)

    @pl.when(i < lc // tq)
    def _():
        run(lc)

    @pl.when(i >= lc // tq)
    def _():
        run(t)


def _attention(qk, rest, *, lc, tq):
    b, t, _ = qk.shape
    kern = functools.partial(_attn_kernel, tq=tq, lc=lc, t=t)
    return pl.pallas_call(
        kern,
        grid=(b, t // tq),
        in_specs=[
            pl.BlockSpec((None, tq, ATTN_WIDTH), lambda bi, i: (bi, i, 0)),
            pl.BlockSpec((None, t, KV_WIDTH), lambda bi, i: (bi, 0, ATTN_WIDTH // KV_WIDTH)),
            pl.BlockSpec((None, t, KV_WIDTH), lambda bi, i: (bi, 0, R_V // KV_WIDTH)),
        ],
        out_specs=pl.BlockSpec((None, tq, ATTN_WIDTH), lambda bi, i: (bi, i, 0)),
        out_shape=jax.ShapeDtypeStruct((b, t, ATTN_WIDTH), BF16),
        compiler_params=_cparams(("arbitrary", "arbitrary")),
        name="attention",
    )(qk, qk, rest)


def _merge_kernel(x_ref, r_ref, rp_ref, rn_ref, a_ref, modb_ref, modc_ref,
                  wao_ref, wco_ref, wso_ref, wmix_ref,
                  cw_ref, cb_ref, lg_ref, lb_ref, sw_ref,
                  o_ref, hp_scr, up_scr, *, tm, lc, t):
    i = pl.program_id(1)
    n_ctx = lc // tm
    n_all = t // tm
    prev_ok = jnp.logical_and(i != 0, i != n_ctx)
    next_ok = jnp.logical_and(i != n_ctx - 1, i != n_all - 1)

    def glu(ref, rows):
        a = ref[rows, R_CONF_A:R_CONF_A + CONF_WIDTH].astype(F32)
        g = ref[rows, R_CONF_G:R_CONF_G + CONF_WIDTH].astype(F32)
        return a * _sigmoid(g)

    def cx(ref, rows):
        c = ref[rows, R_SC_C:R_SC_C + SC_WIDTH].astype(F32)
        x = ref[rows, R_SC_X:R_SC_X + SC_WIDTH].astype(F32)
        return c * x

    halo = slice(0, HALO)
    main = slice(0, tm)
    hp_scr[0:HALO, :] = jnp.where(prev_ok, glu(rp_ref, halo), 0.0)
    hp_scr[HALO:HALO + tm, :] = glu(r_ref, main)
    hp_scr[HALO + tm:, :] = jnp.where(next_ok, glu(rn_ref, halo), 0.0)
    up_scr[0:HALO, :] = jnp.where(prev_ok, cx(rp_ref, halo), 0.0)
    up_scr[HALO:HALO + tm, :] = cx(r_ref, main)
    up_scr[HALO + tm:, :] = jnp.where(next_ok, cx(rn_ref, halo), 0.0)

    conv_cols = []
    for c0 in range(0, CONF_WIDTH, LANES):
        acc = jnp.zeros((tm, LANES), F32)
        for k in range(CONF_KERNEL):
            off = HALO - CONF_KERNEL // 2 + k
            acc = acc + cw_ref[k:k + 1, c0:c0 + LANES] * hp_scr[off:off + tm, c0:c0 + LANES]
        conv_cols.append(acc)
    hc = jnp.concatenate(conv_cols, axis=1) + cb_ref[...]
    mu = jnp.mean(hc, axis=-1, keepdims=True)
    var = jnp.mean(jnp.square(hc - mu), axis=-1, keepdims=True)
    hn = (hc - mu) * lax.rsqrt(var + EPS) * lg_ref[...] + lb_ref[...]
    hn = hn * _sigmoid(hn)
    y_conf = jnp.dot(hn.astype(BF16), wco_ref[...], preferred_element_type=F32)

    sacc = jnp.zeros((tm, SC_WIDTH), F32)
    for k in range(SC_KERNEL):
        off = HALO - SC_KERNEL // 2 + k
        sacc = sacc + sw_ref[k:k + 1, :] * up_scr[off:off + tm, :]
    bg = r_ref[:, R_SC_B:R_SC_B + SC_WIDTH].astype(F32)
    y_sc = jnp.dot((bg * sacc).astype(BF16), wso_ref[...], preferred_element_type=F32)

    y_attn = jnp.dot(a_ref[...], wao_ref[...], preferred_element_type=F32)

    def gate(m):
        return _sigmoid(r_ref[:, R_GATE + m * D_MODEL:R_GATE + (m + 1) * D_MODEL].astype(F32))

    merged = gate(0) * y_attn + gate(1) * y_conf + gate(2) * y_sc
    mixed = jnp.dot(merged.astype(BF16), wmix_ref[...], preferred_element_type=F32)
    (g_res,) = _mod_rows(i, tm, lc, modb_ref, modc_ref, (2,))
    o_ref[...] = x_ref[...] + g_res * mixed


def _merge(xs, rest, attn, mod_l, wao, wco, wso, wmix, cw, cb, lg, lb, sw, *, lc, tm):
    b, t, d = xs.shape
    nb = mod_l.shape[0] - 1
    hb = tm // HALO
    n_hblk = t // HALO
    kern = functools.partial(_merge_kernel, tm=tm, lc=lc, t=t)
    const2 = lambda bi, i: (0, 0)
    return pl.pallas_call(
        kern,
        grid=(b, t // tm),
        in_specs=[
            pl.BlockSpec((None, tm, d), lambda bi, i: (bi, i, 0)),
            pl.BlockSpec((None, tm, REST_WIDTH), lambda bi, i: (bi, i, 0)),
            pl.BlockSpec((None, HALO, REST_WIDTH), lambda bi, i: (bi, jnp.maximum(i * hb - 1, 0), 0)),
            pl.BlockSpec((None, HALO, REST_WIDTH), lambda bi, i: (bi, jnp.minimum((i + 1) * hb, n_hblk - 1), 0)),
            pl.BlockSpec((None, tm, ATTN_WIDTH), lambda bi, i: (bi, i, 0)),
            pl.BlockSpec((None, N_MOD, d), lambda bi, i: (bi, 0, 0)),
            pl.BlockSpec((None, N_MOD, d), lambda bi, i: (nb, 0, 0)),
            pl.BlockSpec((ATTN_WIDTH, d), const2),
            pl.BlockSpec((CONF_WIDTH, d), const2),
            pl.BlockSpec((SC_WIDTH, d), const2),
            pl.BlockSpec((d, d), const2),
            pl.BlockSpec((CONF_KERNEL, CONF_WIDTH), const2),
            pl.BlockSpec((1, CONF_WIDTH), const2),
            pl.BlockSpec((1, CONF_WIDTH), const2),
            pl.BlockSpec((1, CONF_WIDTH), const2),
            pl.BlockSpec((SC_KERNEL, SC_WIDTH), const2),
        ],
        out_specs=pl.BlockSpec((None, tm, d), lambda bi, i: (bi, i, 0)),
        out_shape=jax.ShapeDtypeStruct((b, t, d), F32),
        scratch_shapes=[pltpu.VMEM((tm + 2 * HALO, CONF_WIDTH), F32),
                        pltpu.VMEM((tm + 2 * HALO, SC_WIDTH), F32)],
        compiler_params=_cparams(("arbitrary", "arbitrary")),
        name="merge",
    )(xs, rest, rest, rest, attn, mod_l, mod_l, wao, wco, wso, wmix, cw, cb, lg, lb, sw)


def _ffn_kernel(x_ref, modb_ref, modc_ref, wa_ref, wb_ref, wo_ref, o_ref, h_scr, acc_scr, *, tm, lc):
    i = pl.program_id(1)
    j = pl.program_id(2)

    @pl.when(j == 0)
    def _():
        shift, scale = _mod_rows(i, tm, lc, modb_ref, modc_ref, (3, 4))
        h_scr[...] = _modulated_rms(x_ref[...], shift, scale).astype(BF16)
        acc_scr[...] = jnp.zeros_like(acc_scr)

    h = h_scr[...]
    a = jnp.dot(h, wa_ref[...], preferred_element_type=F32)
    bb = jnp.dot(h, wb_ref[...], preferred_element_type=F32)
    u = (a * _sigmoid(a) * bb).astype(BF16)
    acc_scr[...] += jnp.dot(u, wo_ref[...], preferred_element_type=F32)

    @pl.when(j == pl.num_programs(2) - 1)
    def _():
        (g_res,) = _mod_rows(i, tm, lc, modb_ref, modc_ref, (5,))
        o_ref[...] = x_ref[...] + g_res * acc_scr[...]


def _ffn(xs, mod_l, w_in, w_out, *, lc, tm, th):
    b, t, d = xs.shape
    nb = mod_l.shape[0] - 1
    nh = FFN_HIDDEN // th
    kern = functools.partial(_ffn_kernel, tm=tm, lc=lc)
    return pl.pallas_call(
        kern,
        grid=(b, t // tm, nh),
        in_specs=[
            pl.BlockSpec((None, tm, d), lambda bi, i, j: (bi, i, 0)),
            pl.BlockSpec((None, N_MOD, d), lambda bi, i, j: (bi, 0, 0)),
            pl.BlockSpec((None, N_MOD, d), lambda bi, i, j: (nb, 0, 0)),
            pl.BlockSpec((d, th), lambda bi, i, j: (0, j)),
            pl.BlockSpec((d, th), lambda bi, i, j: (0, nh + j)),
            pl.BlockSpec((th, d), lambda bi, i, j: (j, 0)),
        ],
        out_specs=pl.BlockSpec((None, tm, d), lambda bi, i, j: (bi, i, 0)),
        out_shape=jax.ShapeDtypeStruct((b, t, d), F32),
        scratch_shapes=[pltpu.VMEM((tm, d), BF16), pltpu.VMEM((tm, d), F32)],
        compiler_params=_cparams(("arbitrary", "arbitrary", "arbitrary")),
        name="ffn",
    )(xs, mod_l, mod_l, w_in, w_in, w_out)


def _rope_tables(s, lc):
    pos = jnp.arange(s)
    r_ids = (pos // GRID_W).astype(F32)
    c_ids = (pos % GRID_W).astype(F32)
    freqs = ROPE_THETA ** (-jnp.arange(0, AXIS_DIM, 2, dtype=F32) / AXIS_DIM)
    ang_r = r_ids[:, None] * freqs
    ang_c = c_ids[:, None] * freqs
    cos_h = jnp.concatenate([jnp.cos(ang_r), jnp.cos(ang_r), jnp.cos(ang_c), jnp.cos(ang_c)], axis=1)
    sin_h = jnp.concatenate([-jnp.sin(ang_r), jnp.sin(ang_r), -jnp.sin(ang_c), jnp.sin(ang_c)], axis=1)
    reps = LANES // HEAD_DIM
    cos_t = jnp.concatenate([jnp.ones((lc, LANES), F32), jnp.tile(cos_h, (1, reps))], axis=0)
    sin_t = jnp.concatenate([jnp.zeros((lc, LANES), F32), jnp.tile(sin_h, (1, reps))], axis=0)
    return cos_t, sin_t


_HEAD_ORDER = [h for p in range(Q_GROUP) for h in (p, Q_GROUP + p)]


def kernel(x, c, ctx, c_ctx, w_ada, b_ada, w_in, q_norm, k_norm, w_attn_o, conf_dw_w, conf_dw_b,
           conf_ln_g, conf_ln_b, w_conf_out, sc_dw_w, w_sc_out, w_mix_out, w_ffn_in, w_ffn_out):
    b, s, d = x.shape
    lc = ctx.shape[1]
    n_layers = w_ada.shape[0]
    assert d == D_MODEL and lc % 256 == 0 and s % 256 == 0

    xs = jnp.concatenate([ctx, x], axis=1)
    t = lc + s

    mp = -(-(b + 1) // 8) * 8
    cc = jnp.concatenate([c, jnp.zeros((mp - b - 1, d), F32), c_ctx[None, :]], axis=0)
    mod = _modulation(cc, w_ada, b_ada).reshape(n_layers, mp, N_MOD, d)

    cos_t, sin_t = _rope_tables(s, lc)
    head_of_lane = jnp.arange(QK_WIDTH) // HEAD_DIM
    seg = (head_of_lane[:, None] == head_of_lane[None, :]).astype(BF16) * (1.0 / HEAD_DIM)
    q_cols = jnp.concatenate([jnp.arange(h * HEAD_DIM, (h + 1) * HEAD_DIM) for h in _HEAD_ORDER])

    for i in range(n_layers):
        wi = w_in[i]
        w_perm = jnp.concatenate([wi[:, q_cols], wi[:, OFF_K:OFF_V], wi[:, OFF_V:]], axis=1).astype(BF16)
        gvec = jnp.concatenate([jnp.tile(q_norm[i] * ATTN_SCALE, N_Q_HEADS), jnp.tile(k_norm[i], N_KV_HEADS)])[None, :]
        qk, rest = _in_proj(xs, mod[i], w_perm, gvec, cos_t, sin_t, seg, lc=lc, tm=768)
        attn = _attention(qk, rest, lc=lc, tq=256)
        xs = _merge(xs, rest, attn, mod[i],
                    w_attn_o[i][q_cols, :].astype(BF16), w_conf_out[i].astype(BF16),
                    w_sc_out[i].astype(BF16), w_mix_out[i].astype(BF16),
                    conf_dw_w[i], conf_dw_b[i][None, :], conf_ln_g[i][None, :], conf_ln_b[i][None, :],
                    sc_dw_w[i], lc=lc, tm=256)
        xs = _ffn(xs, mod[i], w_ffn_in[i].astype(BF16), w_ffn_out[i].astype(BF16), lc=lc, tm=768, th=256)
    return xs[:, lc:, :]
```

```python
import functools

import jax
import jax.numpy as jnp
from jax import lax
from jax.experimental import pallas as pl
from jax.experimental.pallas import tpu as pltpu

F32 = jnp.float32
BF16 = jnp.bfloat16

D_MODEL = 1024
HEAD_DIM = 64
N_Q_HEADS = 8
N_KV_HEADS = 2
Q_GROUP = N_Q_HEADS // N_KV_HEADS
ATTN_WIDTH = N_Q_HEADS * HEAD_DIM
KV_WIDTH = N_KV_HEADS * HEAD_DIM
AXIS_DIM = HEAD_DIM // 2
ROPE_THETA = 10000.0
GRID_W = 64
ATTN_SCALE = HEAD_DIM ** -0.5
CONF_WIDTH = D_MODEL // 2
CONF_KERNEL = 31
SC_WIDTH = D_MODEL // 2
SC_KERNEL = 3
N_BRANCHES = 3
N_MOD = 6
FFN_HIDDEN = 2816
EPS = 1e-6

OFF_K = ATTN_WIDTH
OFF_V = OFF_K + KV_WIDTH
OFF_CONF = OFF_V + KV_WIDTH
OFF_SC = OFF_CONF + 2 * CONF_WIDTH
OFF_GATE = OFF_SC + 3 * SC_WIDTH
IN_WIDTH = OFF_GATE + N_BRANCHES * D_MODEL

LANES = 128
SUBLANES = 8
MXU_N = 256
QK_WIDTH = ATTN_WIDTH + KV_WIDTH
QKV_WIDTH = QK_WIDTH + KV_WIDTH
MID_WIDTH = CONF_WIDTH + 2 * SC_WIDTH
M_GLU = 0
M_CX = CONF_WIDTH
M_BG = CONF_WIDTH + SC_WIDTH
GATE_WIDTH = N_BRANCHES * D_MODEL
W_QKV = 0
W_CONF = QKV_WIDTH
W_CX = W_CONF + 2 * CONF_WIDTH
W_BG = W_CX + 2 * SC_WIDTH
W_GATE = W_BG + SC_WIDTH
HALO = 16
FFN_TH = 2 * MXU_N
VMEM_LIMIT = 56 * 1024 * 1024


def _cparams(sem):
    return pltpu.CompilerParams(dimension_semantics=sem, vmem_limit_bytes=VMEM_LIMIT)


def _resident(shape):
    return pl.BlockSpec(shape, lambda *_: (0,) * len(shape), pipeline_mode=pl.Buffered(1))


def _sigmoid(x):
    return 1.0 / (1.0 + jnp.exp(-x))


def _mod_kernel(c_ref, w_ref, b_ref, o_ref):
    a = c_ref[...]
    a = (a * _sigmoid(a)).astype(BF16)
    o_ref[...] = jnp.dot(a, w_ref[...].astype(BF16), preferred_element_type=F32) + b_ref[...]


def _modulation(cc, w_ada, b_ada):
    n_layers, d, n = w_ada.shape
    mp = cc.shape[0]
    tn = 1536
    return pl.pallas_call(
        _mod_kernel,
        grid=(n_layers, n // tn),
        in_specs=[
            pl.BlockSpec((mp, d), lambda l, j: (0, 0)),
            pl.BlockSpec((None, d, tn), lambda l, j: (l, 0, j)),
            pl.BlockSpec((None, 1, tn), lambda l, j: (l, 0, j)),
        ],
        out_specs=pl.BlockSpec((None, mp, tn), lambda l, j: (l, 0, j)),
        out_shape=jax.ShapeDtypeStruct((n_layers, mp, n), F32),
        compiler_params=_cparams(("arbitrary", "arbitrary")),
        name="adaln_mod",
    )(cc, w_ada, b_ada.reshape(n_layers, 1, n))


def _mod_rows(i, tm, lc, modb_ref, modc_ref, idx):
    row = i * tm + lax.broadcasted_iota(jnp.int32, (tm, 1), 0)
    is_ctx = row < lc
    return [jnp.where(is_ctx, modc_ref[k:k + 1, :], modb_ref[k:k + 1, :]) for k in idx]


def _modulated_rms(x, shift, scale):
    ms = jnp.mean(x * x, axis=-1, keepdims=True)
    return (x * lax.rsqrt(ms + EPS)) * (1.0 + scale) + shift


def _store_modulated_rms(h_scr, x_ref, i, tm, lc, modb_ref, modc_ref, k_shift, k_scale):
    @pl.when(i * tm < lc)
    def _():
        shift, scale = _mod_rows(i, tm, lc, modb_ref, modc_ref, (k_shift, k_scale))
        h_scr[...] = _modulated_rms(x_ref[...], shift, scale).astype(BF16)

    @pl.when(i * tm >= lc)
    def _():
        h_scr[...] = _modulated_rms(x_ref[...], modb_ref[k_shift:k_shift + 1, :],
                                    modb_ref[k_scale:k_scale + 1, :]).astype(BF16)


def _inproj_kernel(x_ref, modb_ref, modc_ref, w_ref, g_ref, cos_ref, sin_ref, seg_ref,
                   oqkv_ref, omid_ref, ogate_ref, h_scr, *, tm, lc):
    i = pl.program_id(1)
    _store_modulated_rms(h_scr, x_ref, i, tm, lc, modb_ref, modc_ref, 0, 1)

    def proj(c0, width):
        return jnp.dot(h_scr[...], w_ref[:, c0:c0 + width], preferred_element_type=F32)

    acc = proj(W_QKV, QKV_WIDTH)
    qk = acc[:, :QK_WIDTH]
    ms = jnp.dot((qk * qk).astype(BF16), seg_ref[...], preferred_element_type=F32)
    y = qk * lax.rsqrt(ms + EPS) * g_ref[...]
    cos = cos_ref[...]
    sin = sin_ref[...]
    lane = lax.broadcasted_iota(jnp.int32, (tm, LANES), 1)
    first = (lane % AXIS_DIM) < (AXIS_DIM // 2)
    for s in range(QK_WIDTH // LANES):
        ys = y[:, s * LANES:(s + 1) * LANES]
        partner = jnp.where(first, pltpu.roll(ys, LANES - AXIS_DIM // 2, 1), pltpu.roll(ys, AXIS_DIM // 2, 1))
        oqkv_ref[:, s * LANES:(s + 1) * LANES] = (ys * cos + partner * sin).astype(BF16)
    oqkv_ref[:, QK_WIDTH:] = acc[:, QK_WIDTH:].astype(BF16)

    acc = proj(W_CONF, 2 * CONF_WIDTH)
    omid_ref[:, M_GLU:M_GLU + CONF_WIDTH] = (acc[:, :CONF_WIDTH] * _sigmoid(acc[:, CONF_WIDTH:])).astype(BF16)
    acc = proj(W_CX, 2 * SC_WIDTH)
    omid_ref[:, M_CX:M_CX + SC_WIDTH] = (acc[:, :SC_WIDTH] * acc[:, SC_WIDTH:]).astype(BF16)
    omid_ref[:, M_BG:M_BG + SC_WIDTH] = proj(W_BG, SC_WIDTH).astype(BF16)
    for m in range(N_BRANCHES):
        acc = proj(W_GATE + m * D_MODEL, D_MODEL)
        ogate_ref[:, m * D_MODEL:(m + 1) * D_MODEL] = _sigmoid(acc).astype(BF16)


def _in_proj(xs, mod_l, w, gvec, cos_t, sin_t, seg, *, lc, tm):
    b, t, d = xs.shape
    nb = mod_l.shape[0] - 1
    kern = functools.partial(_inproj_kernel, tm=tm, lc=lc)
    tok = lambda width: pl.BlockSpec((None, tm, width), lambda bi, i: (bi, i, 0))
    return pl.pallas_call(
        kern,
        grid=(b, t // tm),
        in_specs=[
            tok(d),
            pl.BlockSpec((None, N_MOD, d), lambda bi, i: (bi, 0, 0)),
            pl.BlockSpec((None, N_MOD, d), lambda bi, i: (nb, 0, 0)),
            _resident((d, IN_WIDTH)),
            _resident((1, QK_WIDTH)),
            pl.BlockSpec((tm, LANES), lambda bi, i: (i, 0)),
            pl.BlockSpec((tm, LANES), lambda bi, i: (i, 0)),
            _resident((QK_WIDTH, QK_WIDTH)),
        ],
        out_specs=[tok(QKV_WIDTH), tok(MID_WIDTH), tok(GATE_WIDTH)],
        out_shape=[
            jax.ShapeDtypeStruct((b, t, QKV_WIDTH), BF16),
            jax.ShapeDtypeStruct((b, t, MID_WIDTH), BF16),
            jax.ShapeDtypeStruct((b, t, GATE_WIDTH), BF16),
        ],
        scratch_shapes=[pltpu.VMEM((tm, d), BF16)],
        compiler_params=_cparams(("arbitrary", "arbitrary")),
        name="in_proj",
    )(xs, mod_l, mod_l, w, gvec, cos_t, sin_t, seg)


def _attn_kernel(q_ref, k_ref, v_ref, o_ref, *, tq, lc, t):
    i = pl.program_id(1)

    def run(nk):
        k = k_ref[0:nk, :]
        v = v_ref[0:nk, :]
        lane = lax.broadcasted_iota(jnp.int32, (tq, LANES), 1)
        lo = lane < HEAD_DIM
        for p in range(ATTN_WIDTH // LANES):
            qt = q_ref[:, p * LANES:(p + 1) * LANES]
            outs = []
            for mask in (lo, jnp.logical_not(lo)):
                qm = jnp.where(mask, qt, jnp.zeros_like(qt))
                s = lax.dot_general(qm, k, (((1,), (1,)), ((), ())), preferred_element_type=F32)
                m = jnp.max(s, axis=-1, keepdims=True)
                e = jnp.exp(s - m)
                l = jnp.sum(e, axis=-1, keepdims=True)
                o = jnp.dot(e.astype(BF16), v, preferred_element_type=F32)
                outs.append(o / l)
            o_ref[:, p * LANES:(p + 1) * LANES] = jnp.where(lo, outs[0], outs[1]).astype(BF16)

    @pl.when(i < lc // tq)
    def _():
        run(lc)

    @pl.when(i >= lc // tq)
    def _():
        run(t)


def _attention(qkv, *, lc, tq):
    b, t, _ = qkv.shape
    kern = functools.partial(_attn_kernel, tq=tq, lc=lc, t=t)
    return pl.pallas_call(
        kern,
        grid=(b, t // tq),
        in_specs=[
            pl.BlockSpec((None, tq, ATTN_WIDTH), lambda bi, i: (bi, i, 0)),
            pl.BlockSpec((None, t, KV_WIDTH), lambda bi, i: (bi, 0, ATTN_WIDTH // KV_WIDTH)),
            pl.BlockSpec((None, t, KV_WIDTH), lambda bi, i: (bi, 0, QK_WIDTH // KV_WIDTH)),
        ],
        out_specs=pl.BlockSpec((None, tq, ATTN_WIDTH), lambda bi, i: (bi, i, 0)),
        out_shape=jax.ShapeDtypeStruct((b, t, ATTN_WIDTH), BF16),
        compiler_params=_cparams(("arbitrary", "arbitrary")),
        name="attention",
    )(qkv, qkv, qkv)


def _merge_kernel(x_ref, m_ref, mp_ref, mn_ref, gt_ref, a_ref, modb_ref, modc_ref,
                  wao_ref, wco_ref, wso_ref, wmix_ref,
                  cw_ref, cb_ref, lg_ref, lb_ref, sw_ref,
                  o_ref, hp_scr, up_scr, hc_scr, sh_scr, *, tm, lc, t):
    i = pl.program_id(1)
    n_ctx = lc // tm
    n_all = t // tm
    prev_ok = jnp.logical_and(i != 0, i != n_ctx)
    next_ok = jnp.logical_and(i != n_ctx - 1, i != n_all - 1)

    for scr, c0 in ((hp_scr, M_GLU), (up_scr, M_CX)):
        scr[0:HALO, :] = jnp.where(prev_ok, mp_ref[:, c0:c0 + CONF_WIDTH].astype(F32), 0.0)
        scr[HALO:HALO + tm, :] = m_ref[:, c0:c0 + CONF_WIDTH].astype(F32)
        scr[HALO + tm:, :] = jnp.where(next_ok, mn_ref[:, c0:c0 + CONF_WIDTH].astype(F32), 0.0)

    rows = 128
    first_off = HALO - CONF_KERNEL // 2
    reach = (first_off + CONF_KERNEL - 1) // SUBLANES * SUBLANES
    for c0 in range(0, CONF_WIDTH, LANES):
        for r in range(1, SUBLANES):
            sh_scr[r - 1] = hp_scr[r:r + tm + reach, c0:c0 + LANES]
        for r0 in range(0, tm, rows):
            acc = jnp.zeros((rows, LANES), F32)
            for k in range(CONF_KERNEL):
                r = (first_off + k) % SUBLANES
                a8 = r0 + (first_off + k) // SUBLANES * SUBLANES
                src = hp_scr[a8:a8 + rows, c0:c0 + LANES] if r == 0 else sh_scr[r - 1, a8:a8 + rows, :]
                acc = acc + cw_ref[k:k + 1, c0:c0 + LANES] * src
            hc_scr[r0:r0 + rows, c0:c0 + LANES] = acc
    hc = hc_scr[...] + cb_ref[...]
    mu = jnp.mean(hc, axis=-1, keepdims=True)
    var = jnp.mean(jnp.square(hc - mu), axis=-1, keepdims=True)
    hn = (hc - mu) * lax.rsqrt(var + EPS) * lg_ref[...] + lb_ref[...]
    hn = hn * _sigmoid(hn)
    y_conf = jnp.dot(hn.astype(BF16), wco_ref[...], preferred_element_type=F32)

    sacc = jnp.zeros((tm, SC_WIDTH), F32)
    for k in range(SC_KERNEL):
        off = HALO - SC_KERNEL // 2 + k
        sacc = sacc + sw_ref[k:k + 1, :] * up_scr[off:off + tm, :]
    bg = m_ref[:, M_BG:M_BG + SC_WIDTH].astype(F32)
    y_sc = jnp.dot((bg * sacc).astype(BF16), wso_ref[...], preferred_element_type=F32)

    y_attn = jnp.dot(a_ref[...], wao_ref[...], preferred_element_type=F32)

    def gate(m):
        return gt_ref[:, m * D_MODEL:(m + 1) * D_MODEL].astype(F32)

    merged = gate(0) * y_attn + gate(1) * y_conf + gate(2) * y_sc
    mixed = jnp.dot(merged.astype(BF16), wmix_ref[...], preferred_element_type=F32)
    (g_res,) = _mod_rows(i, tm, lc, modb_ref, modc_ref, (2,))
    o_ref[...] = x_ref[...] + g_res * mixed


def _merge(xs, mid, gates, attn, mod_l, wao, wco, wso, wmix, cw, cb, lg, lb, sw, *, lc, tm):
    b, t, d = xs.shape
    nb = mod_l.shape[0] - 1
    hb = tm // HALO
    n_hblk = t // HALO
    conv_in = CONF_WIDTH + SC_WIDTH
    kern = functools.partial(_merge_kernel, tm=tm, lc=lc, t=t)
    tok = lambda width: pl.BlockSpec((None, tm, width), lambda bi, i: (bi, i, 0))
    return pl.pallas_call(
        kern,
        grid=(b, t // tm),
        in_specs=[
            tok(d),
            tok(MID_WIDTH),
            pl.BlockSpec((None, HALO, conv_in), lambda bi, i: (bi, jnp.maximum(i * hb - 1, 0), 0)),
            pl.BlockSpec((None, HALO, conv_in), lambda bi, i: (bi, jnp.minimum((i + 1) * hb, n_hblk - 1), 0)),
            tok(GATE_WIDTH),
            tok(ATTN_WIDTH),
            pl.BlockSpec((None, N_MOD, d), lambda bi, i: (bi, 0, 0)),
            pl.BlockSpec((None, N_MOD, d), lambda bi, i: (nb, 0, 0)),
            _resident((ATTN_WIDTH, d)),
            _resident((CONF_WIDTH, d)),
            _resident((SC_WIDTH, d)),
            _resident((d, d)),
            _resident((CONF_KERNEL, CONF_WIDTH)),
            _resident((1, CONF_WIDTH)),
            _resident((1, CONF_WIDTH)),
            _resident((1, CONF_WIDTH)),
            _resident((SC_KERNEL, SC_WIDTH)),
        ],
        out_specs=tok(d),
        out_shape=jax.ShapeDtypeStruct((b, t, d), F32),
        scratch_shapes=[pltpu.VMEM((tm + 2 * HALO, CONF_WIDTH), F32),
                        pltpu.VMEM((tm + 2 * HALO, SC_WIDTH), F32),
                        pltpu.VMEM((tm, CONF_WIDTH), F32),
                        pltpu.VMEM((SUBLANES - 1, tm + 2 * HALO - SUBLANES, LANES), F32)],
        compiler_params=_cparams(("arbitrary", "arbitrary")),
        name="merge",
    )(xs, mid, mid, mid, gates, attn, mod_l, mod_l, wao, wco, wso, wmix, cw, cb, lg, lb, sw)


def _ffn_kernel(x_ref, modb_ref, modc_ref, wi_ref, wo_ref, o_ref, h_scr, u_scr, *, tm, lc):
    i = pl.program_id(1)
    _store_modulated_rms(h_scr, x_ref, i, tm, lc, modb_ref, modc_ref, 3, 4)
    half = FFN_TH // 2
    for j in range(2 * FFN_HIDDEN // FFN_TH):
        ab = jnp.dot(h_scr[...], wi_ref[:, j * FFN_TH:(j + 1) * FFN_TH], preferred_element_type=F32)
        a = ab[:, :half]
        u_scr[:, j * half:(j + 1) * half] = (a * _sigmoid(a) * ab[:, half:]).astype(BF16)
    y = jnp.dot(u_scr[...], wo_ref[...], preferred_element_type=F32)
    (g_res,) = _mod_rows(i, tm, lc, modb_ref, modc_ref, (5,))
    o_ref[...] = x_ref[...] + g_res * y


def _ffn(xs, mod_l, w_in, w_out, *, lc, tm):
    b, t, d = xs.shape
    nb = mod_l.shape[0] - 1
    kern = functools.partial(_ffn_kernel, tm=tm, lc=lc)
    tok = pl.BlockSpec((None, tm, d), lambda bi, i: (bi, i, 0))
    return pl.pallas_call(
        kern,
        grid=(b, t // tm),
        in_specs=[
            tok,
            pl.BlockSpec((None, N_MOD, d), lambda bi, i: (bi, 0, 0)),
            pl.BlockSpec((None, N_MOD, d), lambda bi, i: (nb, 0, 0)),
            _resident((d, 2 * FFN_HIDDEN)),
            _resident((FFN_HIDDEN, d)),
        ],
        out_specs=tok,
        out_shape=jax.ShapeDtypeStruct((b, t, d), F32),
        scratch_shapes=[pltpu.VMEM((tm, d), BF16), pltpu.VMEM((tm, FFN_HIDDEN), BF16)],
        compiler_params=_cparams(("arbitrary", "arbitrary")),
        name="ffn",
    )(xs, mod_l, mod_l, w_in, w_out)


def _rope_tables(s, lc):
    pos = jnp.arange(s)
    r_ids = (pos // GRID_W).astype(F32)
    c_ids = (pos % GRID_W).astype(F32)
    freqs = ROPE_THETA ** (-jnp.arange(0, AXIS_DIM, 2, dtype=F32) / AXIS_DIM)
    ang_r = r_ids[:, None] * freqs
    ang_c = c_ids[:, None] * freqs
    cos_h = jnp.concatenate([jnp.cos(ang_r), jnp.cos(ang_r), jnp.cos(ang_c), jnp.cos(ang_c)], axis=1)
    sin_h = jnp.concatenate([-jnp.sin(ang_r), jnp.sin(ang_r), -jnp.sin(ang_c), jnp.sin(ang_c)], axis=1)
    reps = LANES // HEAD_DIM
    cos_t = jnp.concatenate([jnp.ones((lc, LANES), F32), jnp.tile(cos_h, (1, reps))], axis=0)
    sin_t = jnp.concatenate([jnp.zeros((lc, LANES), F32), jnp.tile(sin_h, (1, reps))], axis=0)
    return cos_t, sin_t


_HEAD_ORDER = [h for p in range(Q_GROUP) for h in (p, Q_GROUP + p)]


def kernel(x, c, ctx, c_ctx, w_ada, b_ada, w_in, q_norm, k_norm, w_attn_o, conf_dw_w, conf_dw_b,
           conf_ln_g, conf_ln_b, w_conf_out, sc_dw_w, w_sc_out, w_mix_out, w_ffn_in, w_ffn_out):
    b, s, d = x.shape
    lc = ctx.shape[1]
    n_layers = w_ada.shape[0]
    assert d == D_MODEL and lc % 256 == 0 and s % 256 == 0

    xs = jnp.concatenate([ctx, x], axis=1)

    mp = -(-(b + 1) // SUBLANES) * SUBLANES
    cc = jnp.concatenate([c, jnp.zeros((mp - b - 1, d), F32), c_ctx[None, :]], axis=0)
    mod = _modulation(cc, w_ada, b_ada).reshape(n_layers, mp, N_MOD, d)

    cos_t, sin_t = _rope_tables(s, lc)
    head_of_lane = jnp.arange(QK_WIDTH) // HEAD_DIM
    seg = (head_of_lane[:, None] == head_of_lane[None, :]).astype(BF16) * (1.0 / HEAD_DIM)
    q_cols = jnp.concatenate([jnp.arange(h * HEAD_DIM, (h + 1) * HEAD_DIM) for h in _HEAD_ORDER])
    sc_b, sc_cx = slice(OFF_SC, OFF_SC + SC_WIDTH), slice(OFF_SC + SC_WIDTH, OFF_GATE)
    n_ffn_tiles = 2 * FFN_HIDDEN // FFN_TH

    for i in range(n_layers):
        wi = w_in[i]
        w_perm = jnp.concatenate([wi[:, q_cols], wi[:, OFF_K:OFF_SC], wi[:, sc_cx], wi[:, sc_b], wi[:, OFF_GATE:]],
                                 axis=1).astype(BF16)
        gvec = jnp.concatenate([jnp.tile(q_norm[i] * ATTN_SCALE, N_Q_HEADS), jnp.tile(k_norm[i], N_KV_HEADS)])[None, :]
        w_ffn_ab = (w_ffn_in[i].reshape(d, 2, n_ffn_tiles, FFN_TH // 2).transpose(0, 2, 1, 3)
                    .reshape(d, 2 * FFN_HIDDEN).astype(BF16))
        qkv, mid, gates = _in_proj(xs, mod[i], w_perm, gvec, cos_t, sin_t, seg, lc=lc, tm=768)
        attn = _attention(qkv, lc=lc, tq=256)
        xs = _merge(xs, mid, gates, attn, mod[i],
                    w_attn_o[i][q_cols, :].astype(BF16), w_conf_out[i].astype(BF16),
                    w_sc_out[i].astype(BF16), w_mix_out[i].astype(BF16),
                    conf_dw_w[i], conf_dw_b[i][None, :], conf_ln_g[i][None, :], conf_ln_b[i][None, :],
                    sc_dw_w[i], lc=lc, tm=256)
        xs = _ffn(xs, mod[i], w_ffn_ab, w_ffn_out[i].astype(BF16), lc=lc, tm=768)
    return xs[:, lc:, :]
```

```python
import functools

import jax
import jax.numpy as jnp
from jax import lax
from jax.experimental import pallas as pl
from jax.experimental.pallas import tpu as pltpu

F32 = jnp.float32
BF16 = jnp.bfloat16

D_MODEL = 1024
HEAD_DIM = 64
N_Q_HEADS = 8
N_KV_HEADS = 2
Q_GROUP = N_Q_HEADS // N_KV_HEADS
ATTN_WIDTH = N_Q_HEADS * HEAD_DIM
KV_WIDTH = N_KV_HEADS * HEAD_DIM
AXIS_DIM = HEAD_DIM // 2
ROPE_THETA = 10000.0
GRID_W = 64
ATTN_SCALE = HEAD_DIM ** -0.5
CONF_WIDTH = D_MODEL // 2
CONF_KERNEL = 31
SC_WIDTH = D_MODEL // 2
SC_KERNEL = 3
N_BRANCHES = 3
N_MOD = 6
FFN_HIDDEN = 2816
EPS = 1e-6

OFF_K = ATTN_WIDTH
OFF_V = OFF_K + KV_WIDTH
OFF_CONF = OFF_V + KV_WIDTH
OFF_SC = OFF_CONF + 2 * CONF_WIDTH
OFF_GATE = OFF_SC + 3 * SC_WIDTH
IN_WIDTH = OFF_GATE + N_BRANCHES * D_MODEL

LANES = 128
SUBLANES = 8
MXU_N = 256
QK_WIDTH = ATTN_WIDTH + KV_WIDTH
QKV_WIDTH = QK_WIDTH + KV_WIDTH
MID_WIDTH = CONF_WIDTH + 2 * SC_WIDTH
M_GLU = 0
M_CX = CONF_WIDTH
M_BG = CONF_WIDTH + SC_WIDTH
GATE_WIDTH = N_BRANCHES * D_MODEL
HALO = 16
FFN_TH = 2 * MXU_N
LOG2E = 1.4426950408889634
VMEM_LIMIT = 56 * 1024 * 1024


def _cparams(sem):
    return pltpu.CompilerParams(dimension_semantics=sem, vmem_limit_bytes=VMEM_LIMIT)


def _resident(shape):
    return pl.BlockSpec(shape, lambda *_: (0,) * len(shape), pipeline_mode=pl.Buffered(1))


def _sigmoid(x):
    return 1.0 / (1.0 + jnp.exp(-x))


def _mod_kernel(c_ref, w_ref, b_ref, o_ref):
    a = c_ref[...]
    a = (a * _sigmoid(a)).astype(BF16)
    o_ref[...] = jnp.dot(a, w_ref[...].astype(BF16), preferred_element_type=F32) + b_ref[...]


def _modulation(cc, w_ada, b_ada):
    n_layers, d, n = w_ada.shape
    mp = cc.shape[0]
    tn = 1536
    return pl.pallas_call(
        _mod_kernel,
        grid=(n_layers, n // tn),
        in_specs=[
            pl.BlockSpec((mp, d), lambda l, j: (0, 0)),
            pl.BlockSpec((None, d, tn), lambda l, j: (l, 0, j)),
            pl.BlockSpec((None, 1, tn), lambda l, j: (l, 0, j)),
        ],
        out_specs=pl.BlockSpec((None, mp, tn), lambda l, j: (l, 0, j)),
        out_shape=jax.ShapeDtypeStruct((n_layers, mp, n), F32),
        compiler_params=_cparams(("arbitrary", "arbitrary")),
        name="adaln_mod",
    )(cc, w_ada, b_ada.reshape(n_layers, 1, n))


def _mod_rows(i, tm, lc, modb_ref, modc_ref, idx):
    row = i * tm + lax.broadcasted_iota(jnp.int32, (tm, 1), 0)
    is_ctx = row < lc
    return [jnp.where(is_ctx, modc_ref[k:k + 1, :], modb_ref[k:k + 1, :]) for k in idx]


def _modulated_rms(x, shift, scale):
    ms = jnp.mean(x * x, axis=-1, keepdims=True)
    return (x * lax.rsqrt(ms + EPS)) * (1.0 + scale) + shift


def _store_modulated_rms(h_scr, x_ref, i, tm, lc, modb_ref, modc_ref, k_shift, k_scale):
    @pl.when(i * tm < lc)
    def _():
        shift, scale = _mod_rows(i, tm, lc, modb_ref, modc_ref, (k_shift, k_scale))
        h_scr[...] = _modulated_rms(x_ref[...], shift, scale).astype(BF16)

    @pl.when(i * tm >= lc)
    def _():
        h_scr[...] = _modulated_rms(x_ref[...], modb_ref[k_shift:k_shift + 1, :],
                                    modb_ref[k_scale:k_scale + 1, :]).astype(BF16)


def _inproj_kernel(x_ref, modb_ref, modc_ref, w_ref, g_ref, cos_ref, sin_ref, seg_ref,
                   oqkv_ref, omid_ref, ogate_ref, h_scr, *, tm, lc):
    i = pl.program_id(1)
    _store_modulated_rms(h_scr, x_ref, i, tm, lc, modb_ref, modc_ref, 0, 1)

    def proj(c0, width):
        return jnp.dot(h_scr[...], w_ref[:, c0:c0 + width], preferred_element_type=F32)

    acc = proj(0, QKV_WIDTH)
    qk = acc[:, :QK_WIDTH]
    ms = jnp.dot((qk * qk).astype(BF16), seg_ref[...], preferred_element_type=F32)
    y = qk * lax.rsqrt(ms + EPS) * g_ref[...]
    cos = cos_ref[...]
    sin = sin_ref[...]
    lane = lax.broadcasted_iota(jnp.int32, (tm, LANES), 1)
    first = (lane % AXIS_DIM) < (AXIS_DIM // 2)
    for s in range(QK_WIDTH // LANES):
        ys = y[:, s * LANES:(s + 1) * LANES]
        partner = jnp.where(first, pltpu.roll(ys, LANES - AXIS_DIM // 2, 1), pltpu.roll(ys, AXIS_DIM // 2, 1))
        oqkv_ref[:, s * LANES:(s + 1) * LANES] = (ys * cos + partner * sin).astype(BF16)
    oqkv_ref[:, QK_WIDTH:] = acc[:, QK_WIDTH:].astype(BF16)

    acc = proj(OFF_CONF, 2 * CONF_WIDTH)
    omid_ref[:, M_GLU:M_GLU + CONF_WIDTH] = (acc[:, :CONF_WIDTH] * _sigmoid(acc[:, CONF_WIDTH:])).astype(BF16)
    acc = proj(OFF_SC, 3 * SC_WIDTH)
    omid_ref[:, M_BG:M_BG + SC_WIDTH] = acc[:, :SC_WIDTH].astype(BF16)
    omid_ref[:, M_CX:M_CX + SC_WIDTH] = (acc[:, SC_WIDTH:2 * SC_WIDTH] * acc[:, 2 * SC_WIDTH:]).astype(BF16)
    for m in range(N_BRANCHES):
        acc = proj(OFF_GATE + m * D_MODEL, D_MODEL)
        ogate_ref[:, m * D_MODEL:(m + 1) * D_MODEL] = _sigmoid(acc).astype(BF16)


def _in_proj(xs, mod_l, w, gvec, cos_t, sin_t, seg, *, lc, tm):
    b, t, d = xs.shape
    nb = mod_l.shape[0] - 1
    kern = functools.partial(_inproj_kernel, tm=tm, lc=lc)
    tok = lambda width: pl.BlockSpec((None, tm, width), lambda bi, i: (bi, i, 0))
    return pl.pallas_call(
        kern,
        grid=(b, t // tm),
        in_specs=[
            tok(d),
            pl.BlockSpec((None, N_MOD, d), lambda bi, i: (bi, 0, 0)),
            pl.BlockSpec((None, N_MOD, d), lambda bi, i: (nb, 0, 0)),
            _resident((d, IN_WIDTH)),
            _resident((1, QK_WIDTH)),
            pl.BlockSpec((tm, LANES), lambda bi, i: (i, 0)),
            pl.BlockSpec((tm, LANES), lambda bi, i: (i, 0)),
            _resident((QK_WIDTH, QK_WIDTH)),
        ],
        out_specs=[tok(QKV_WIDTH), tok(MID_WIDTH), tok(GATE_WIDTH)],
        out_shape=[
            jax.ShapeDtypeStruct((b, t, QKV_WIDTH), BF16),
            jax.ShapeDtypeStruct((b, t, MID_WIDTH), BF16),
            jax.ShapeDtypeStruct((b, t, GATE_WIDTH), BF16),
        ],
        scratch_shapes=[pltpu.VMEM((tm, d), BF16)],
        compiler_params=_cparams(("arbitrary", "arbitrary")),
        name="in_proj",
    )(xs, mod_l, mod_l, w, gvec, cos_t, sin_t, seg)


def _attn_kernel(q_ref, k_ref, v_ref, o_ref, kd_scr, ve_scr, *, tq, lc, t):
    i = pl.program_id(1)

    @pl.when(i == 0)
    def _():
        lo = lax.broadcasted_iota(jnp.int32, (t, LANES), 1) < HEAD_DIM
        for src, dst in ((k_ref, kd_scr), (v_ref, ve_scr)):
            x = src[...].astype(F32)
            xs = pltpu.roll(x, HEAD_DIM, 1)
            dst[0, :, 0:LANES] = jnp.where(lo, x, xs).astype(BF16)
            dst[1, :, 0:LANES] = jnp.where(lo, xs, x).astype(BF16)
        for g in range(N_KV_HEADS):
            ve_scr[g, :, LANES:] = jnp.ones((t, LANES), BF16)

    def run(nk):
        lo = lax.broadcasted_iota(jnp.int32, (tq, LANES), 1) < HEAD_DIM
        for p in range(ATTN_WIDTH // LANES):
            g = (p * LANES // HEAD_DIM) // Q_GROUP
            kd = kd_scr[g, 0:nk, :]
            ve = ve_scr[g, 0:nk, :]
            qt = q_ref[:, p * LANES:(p + 1) * LANES]
            outs = []
            for mask in (lo, jnp.logical_not(lo)):
                qm = jnp.where(mask, qt, jnp.zeros_like(qt))
                s = lax.dot_general(qm, kd, (((1,), (1,)), ((), ())), preferred_element_type=F32)
                m = jnp.max(s, axis=-1, keepdims=True)
                e = jnp.exp2(s - m).astype(BF16)
                pv = jnp.dot(e, ve, preferred_element_type=F32)
                outs.append(pv[:, :LANES] / pv[:, LANES:])
            o_ref[:, p * LANES:(p + 1) * LANES] = jnp.where(lo, outs[0], outs[1]).astype(BF16)

    @pl.when(i < lc // tq)
    def _():
        run(lc)

    @pl.when(i >= lc // tq)
    def _():
        run(t)


def _attention(qkv, *, lc, tq):
    b, t, _ = qkv.shape
    kern = functools.partial(_attn_kernel, tq=tq, lc=lc, t=t)
    return pl.pallas_call(
        kern,
        grid=(b, t // tq),
        in_specs=[
            pl.BlockSpec((None, tq, ATTN_WIDTH), lambda bi, i: (bi, i, 0)),
            pl.BlockSpec((None, t, KV_WIDTH), lambda bi, i: (bi, 0, ATTN_WIDTH // KV_WIDTH)),
            pl.BlockSpec((None, t, KV_WIDTH), lambda bi, i: (bi, 0, QK_WIDTH // KV_WIDTH)),
        ],
        out_specs=pl.BlockSpec((None, tq, ATTN_WIDTH), lambda bi, i: (bi, i, 0)),
        out_shape=jax.ShapeDtypeStruct((b, t, ATTN_WIDTH), BF16),
        scratch_shapes=[pltpu.VMEM((N_KV_HEADS, t, LANES), BF16),
                        pltpu.VMEM((N_KV_HEADS, t, 2 * LANES), BF16)],
        compiler_params=_cparams(("arbitrary", "arbitrary")),
        name="attention",
    )(qkv, qkv, qkv)


def _merge_kernel(x_ref, m_ref, mp_ref, mn_ref, gt_ref, a_ref, modb_ref, modc_ref,
                  wao_ref, wco_ref, wso_ref, wmix_ref,
                  cw_ref, cb_ref, lg_ref, lb_ref, sw_ref,
                  o_ref, hp_scr, up_scr, hc_scr, sh_scr, *, tm, lc, t):
    i = pl.program_id(1)
    n_ctx = lc // tm
    n_all = t // tm
    prev_ok = jnp.logical_and(i != 0, i != n_ctx)
    next_ok = jnp.logical_and(i != n_ctx - 1, i != n_all - 1)

    for scr, c0 in ((hp_scr, M_GLU), (up_scr, M_CX)):
        scr[0:HALO, :] = jnp.where(prev_ok, mp_ref[:, c0:c0 + CONF_WIDTH].astype(F32), 0.0)
        scr[HALO:HALO + tm, :] = m_ref[:, c0:c0 + CONF_WIDTH].astype(F32)
        scr[HALO + tm:, :] = jnp.where(next_ok, mn_ref[:, c0:c0 + CONF_WIDTH].astype(F32), 0.0)

    y_attn = jnp.dot(a_ref[...], wao_ref[...], preferred_element_type=F32)
    sacc = jnp.zeros((tm, SC_WIDTH), F32)
    for k in range(SC_KERNEL):
        off = HALO - SC_KERNEL // 2 + k
        sacc = sacc + sw_ref[k:k + 1, :] * up_scr[off:off + tm, :]
    bg = m_ref[:, M_BG:M_BG + SC_WIDTH].astype(F32)
    y_sc = jnp.dot((bg * sacc).astype(BF16), wso_ref[...], preferred_element_type=F32)

    rows = 128
    first_off = HALO - CONF_KERNEL // 2
    reach = (first_off + CONF_KERNEL - 1) // SUBLANES * SUBLANES
    for c0 in range(0, CONF_WIDTH, LANES):
        for r in range(1, SUBLANES):
            sh_scr[r - 1] = hp_scr[r:r + tm + reach, c0:c0 + LANES]
        for r0 in range(0, tm, rows):
            acc = jnp.zeros((rows, LANES), F32)
            for k in range(CONF_KERNEL):
                r = (first_off + k) % SUBLANES
                a8 = r0 + (first_off + k) // SUBLANES * SUBLANES
                src = hp_scr[a8:a8 + rows, c0:c0 + LANES] if r == 0 else sh_scr[r - 1, a8:a8 + rows, :]
                acc = acc + cw_ref[k:k + 1, c0:c0 + LANES] * src
            hc_scr[r0:r0 + rows, c0:c0 + LANES] = acc
    hc = hc_scr[...] + cb_ref[...]
    mu = jnp.mean(hc, axis=-1, keepdims=True)
    var = jnp.mean(jnp.square(hc - mu), axis=-1, keepdims=True)
    hn = (hc - mu) * lax.rsqrt(var + EPS) * lg_ref[...] + lb_ref[...]
    hn = hn * _sigmoid(hn)
    y_conf = jnp.dot(hn.astype(BF16), wco_ref[...], preferred_element_type=F32)

    def gate(m):
        return gt_ref[:, m * D_MODEL:(m + 1) * D_MODEL].astype(F32)

    merged = gate(0) * y_attn + gate(1) * y_conf + gate(2) * y_sc
    mixed = jnp.dot(merged.astype(BF16), wmix_ref[...], preferred_element_type=F32)
    (g_res,) = _mod_rows(i, tm, lc, modb_ref, modc_ref, (2,))
    o_ref[...] = x_ref[...] + g_res * mixed


def _merge(xs, mid, gates, attn, mod_l, wao, wco, wso, wmix, cw, cb, lg, lb, sw, *, lc, tm):
    b, t, d = xs.shape
    nb = mod_l.shape[0] - 1
    hb = tm // HALO
    n_hblk = t // HALO
    conv_in = CONF_WIDTH + SC_WIDTH
    kern = functools.partial(_merge_kernel, tm=tm, lc=lc, t=t)
    tok = lambda width: pl.BlockSpec((None, tm, width), lambda bi, i: (bi, i, 0))
    return pl.pallas_call(
        kern,
        grid=(b, t // tm),
        in_specs=[
            tok(d),
            tok(MID_WIDTH),
            pl.BlockSpec((None, HALO, conv_in), lambda bi, i: (bi, jnp.maximum(i * hb - 1, 0), 0)),
            pl.BlockSpec((None, HALO, conv_in), lambda bi, i: (bi, jnp.minimum((i + 1) * hb, n_hblk - 1), 0)),
            tok(GATE_WIDTH),
            tok(ATTN_WIDTH),
            pl.BlockSpec((None, N_MOD, d), lambda bi, i: (bi, 0, 0)),
            pl.BlockSpec((None, N_MOD, d), lambda bi, i: (nb, 0, 0)),
            _resident((ATTN_WIDTH, d)),
            _resident((CONF_WIDTH, d)),
            _resident((SC_WIDTH, d)),
            _resident((d, d)),
            _resident((CONF_KERNEL, CONF_WIDTH)),
            _resident((1, CONF_WIDTH)),
            _resident((1, CONF_WIDTH)),
            _resident((1, CONF_WIDTH)),
            _resident((SC_KERNEL, SC_WIDTH)),
        ],
        out_specs=tok(d),
        out_shape=jax.ShapeDtypeStruct((b, t, d), F32),
        scratch_shapes=[pltpu.VMEM((tm + 2 * HALO, CONF_WIDTH), F32),
                        pltpu.VMEM((tm + 2 * HALO, SC_WIDTH), F32),
                        pltpu.VMEM((tm, CONF_WIDTH), F32),
                        pltpu.VMEM((SUBLANES - 1, tm + 2 * HALO - SUBLANES, LANES), F32)],
        compiler_params=_cparams(("arbitrary", "arbitrary")),
        name="merge",
    )(xs, mid, mid, mid, gates, attn, mod_l, mod_l, wao, wco, wso, wmix, cw, cb, lg, lb, sw)


def _ffn_kernel(x_ref, modb_ref, modc_ref, wa_ref, wb_ref, wo_ref, o_ref, h_scr, u_scr, *, tm, lc):
    i = pl.program_id(1)
    _store_modulated_rms(h_scr, x_ref, i, tm, lc, modb_ref, modc_ref, 3, 4)
    for c0 in range(0, FFN_HIDDEN, FFN_TH):
        cols = slice(c0, min(c0 + FFN_TH, FFN_HIDDEN))
        a = jnp.dot(h_scr[...], wa_ref[:, cols], preferred_element_type=F32)
        bb = jnp.dot(h_scr[...], wb_ref[:, cols], preferred_element_type=F32)
        u_scr[:, cols] = (a * _sigmoid(a) * bb).astype(BF16)
    y = jnp.dot(u_scr[...], wo_ref[...], preferred_element_type=F32)
    (g_res,) = _mod_rows(i, tm, lc, modb_ref, modc_ref, (5,))
    o_ref[...] = x_ref[...] + g_res * y


def _ffn(xs, mod_l, w_in, w_out, *, lc, tm):
    b, t, d = xs.shape
    nb = mod_l.shape[0] - 1
    kern = functools.partial(_ffn_kernel, tm=tm, lc=lc)
    tok = pl.BlockSpec((None, tm, d), lambda bi, i: (bi, i, 0))
    return pl.pallas_call(
        kern,
        grid=(b, t // tm),
        in_specs=[
            tok,
            pl.BlockSpec((None, N_MOD, d), lambda bi, i: (bi, 0, 0)),
            pl.BlockSpec((None, N_MOD, d), lambda bi, i: (nb, 0, 0)),
            pl.BlockSpec((d, FFN_HIDDEN), lambda bi, i: (0, 0), pipeline_mode=pl.Buffered(1)),
            pl.BlockSpec((d, FFN_HIDDEN), lambda bi, i: (0, 1), pipeline_mode=pl.Buffered(1)),
            _resident((FFN_HIDDEN, d)),
        ],
        out_specs=tok,
        out_shape=jax.ShapeDtypeStruct((b, t, d), F32),
        scratch_shapes=[pltpu.VMEM((tm, d), BF16), pltpu.VMEM((tm, FFN_HIDDEN), BF16)],
        compiler_params=_cparams(("arbitrary", "arbitrary")),
        name="ffn",
    )(xs, mod_l, mod_l, w_in, w_in, w_out)


def _rope_tables(s, lc):
    pos = jnp.arange(s)
    r_ids = (pos // GRID_W).astype(F32)
    c_ids = (pos % GRID_W).astype(F32)
    freqs = ROPE_THETA ** (-jnp.arange(0, AXIS_DIM, 2, dtype=F32) / AXIS_DIM)
    ang_r = r_ids[:, None] * freqs
    ang_c = c_ids[:, None] * freqs
    cos_h = jnp.concatenate([jnp.cos(ang_r), jnp.cos(ang_r), jnp.cos(ang_c), jnp.cos(ang_c)], axis=1)
    sin_h = jnp.concatenate([-jnp.sin(ang_r), jnp.sin(ang_r), -jnp.sin(ang_c), jnp.sin(ang_c)], axis=1)
    reps = LANES // HEAD_DIM
    cos_t = jnp.concatenate([jnp.ones((lc, LANES), F32), jnp.tile(cos_h, (1, reps))], axis=0)
    sin_t = jnp.concatenate([jnp.zeros((lc, LANES), F32), jnp.tile(sin_h, (1, reps))], axis=0)
    return cos_t, sin_t


def kernel(x, c, ctx, c_ctx, w_ada, b_ada, w_in, q_norm, k_norm, w_attn_o, conf_dw_w, conf_dw_b,
           conf_ln_g, conf_ln_b, w_conf_out, sc_dw_w, w_sc_out, w_mix_out, w_ffn_in, w_ffn_out):
    b, s, d = x.shape
    lc = ctx.shape[1]
    n_layers = w_ada.shape[0]
    assert d == D_MODEL and lc % 256 == 0 and s % 256 == 0

    xs = jnp.concatenate([ctx, x], axis=1)

    mp = -(-(b + 1) // SUBLANES) * SUBLANES
    cc = jnp.concatenate([c, jnp.zeros((mp - b - 1, d), F32), c_ctx[None, :]], axis=0)
    mod = _modulation(cc, w_ada, b_ada).reshape(n_layers, mp, N_MOD, d)

    cos_t, sin_t = _rope_tables(s, lc)
    head_of_lane = jnp.arange(QK_WIDTH) // HEAD_DIM
    seg = (head_of_lane[:, None] == head_of_lane[None, :]).astype(BF16) * (1.0 / HEAD_DIM)
    w_in, w_attn_o, w_conf_out, w_sc_out, w_mix_out, w_ffn_in, w_ffn_out = (
        w.astype(BF16) for w in (w_in, w_attn_o, w_conf_out, w_sc_out, w_mix_out, w_ffn_in, w_ffn_out))

    for i in range(n_layers):
        gvec = jnp.concatenate([jnp.tile(q_norm[i] * (ATTN_SCALE * LOG2E), N_Q_HEADS),
                                jnp.tile(k_norm[i], N_KV_HEADS)])[None, :]
        qkv, mid, gates = _in_proj(xs, mod[i], w_in[i], gvec, cos_t, sin_t, seg, lc=lc, tm=768)
        attn = _attention(qkv, lc=lc, tq=256)
        xs = _merge(xs, mid, gates, attn, mod[i], w_attn_o[i], w_conf_out[i], w_sc_out[i], w_mix_out[i],
                    conf_dw_w[i], conf_dw_b[i][None, :], conf_ln_g[i][None, :], conf_ln_b[i][None, :],
                    sc_dw_w[i], lc=lc, tm=256)
        xs = _ffn(xs, mod[i], w_ffn_in[i], w_ffn_out[i], lc=lc, tm=768)
    return xs[:, lc:, :]
```

```python
import functools

import jax
import jax.numpy as jnp
from jax import lax
from jax.experimental import pallas as pl
from jax.experimental.pallas import tpu as pltpu

F32 = jnp.float32
BF16 = jnp.bfloat16

D_MODEL = 1024
HEAD_DIM = 64
N_Q_HEADS = 8
N_KV_HEADS = 2
Q_GROUP = N_Q_HEADS // N_KV_HEADS
ATTN_WIDTH = N_Q_HEADS * HEAD_DIM
KV_WIDTH = N_KV_HEADS * HEAD_DIM
AXIS_DIM = HEAD_DIM // 2
ROPE_THETA = 10000.0
GRID_W = 64
ATTN_SCALE = HEAD_DIM ** -0.5
CONF_WIDTH = D_MODEL // 2
CONF_KERNEL = 31
SC_WIDTH = D_MODEL // 2
SC_KERNEL = 3
N_BRANCHES = 3
N_MOD = 6
FFN_HIDDEN = 2816
EPS = 1e-6

OFF_K = ATTN_WIDTH
OFF_V = OFF_K + KV_WIDTH
OFF_CONF = OFF_V + KV_WIDTH
OFF_SC = OFF_CONF + 2 * CONF_WIDTH
OFF_GATE = OFF_SC + 3 * SC_WIDTH
IN_WIDTH = OFF_GATE + N_BRANCHES * D_MODEL

LANES = 128
SUBLANES = 8
MXU_N = 256
QK_WIDTH = ATTN_WIDTH + KV_WIDTH
QKV_WIDTH = QK_WIDTH + KV_WIDTH
MID_WIDTH = CONF_WIDTH + 2 * SC_WIDTH
M_GLU = 0
M_CX = CONF_WIDTH
M_BG = CONF_WIDTH + SC_WIDTH
GATE_WIDTH = N_BRANCHES * D_MODEL
HALO = 16
FFN_TH = 2 * MXU_N
ATTN_ROWS = 256
LOG2E = 1.4426950408889634
VMEM_LIMIT = 56 * 1024 * 1024


def _cparams(sem):
    return pltpu.CompilerParams(dimension_semantics=sem, vmem_limit_bytes=VMEM_LIMIT)


def _resident(shape):
    return pl.BlockSpec(shape, lambda *_: (0,) * len(shape), pipeline_mode=pl.Buffered(1))


def _sigmoid(x):
    return 1.0 / (1.0 + jnp.exp(-x))


def _mod_kernel(c_ref, w_ref, b_ref, o_ref):
    a = c_ref[...]
    a = (a * _sigmoid(a)).astype(BF16)
    o_ref[...] = jnp.dot(a, w_ref[...].astype(BF16), preferred_element_type=F32) + b_ref[...]


def _modulation(cc, w_ada, b_ada):
    n_layers, d, n = w_ada.shape
    mp = cc.shape[0]
    tn = 1536
    return pl.pallas_call(
        _mod_kernel,
        grid=(n_layers, n // tn),
        in_specs=[
            pl.BlockSpec((mp, d), lambda l, j: (0, 0)),
            pl.BlockSpec((None, d, tn), lambda l, j: (l, 0, j)),
            pl.BlockSpec((None, 1, tn), lambda l, j: (l, 0, j)),
        ],
        out_specs=pl.BlockSpec((None, mp, tn), lambda l, j: (l, 0, j)),
        out_shape=jax.ShapeDtypeStruct((n_layers, mp, n), F32),
        compiler_params=_cparams(("arbitrary", "arbitrary")),
        name="adaln_mod",
    )(cc, w_ada, b_ada.reshape(n_layers, 1, n))


def _mod_rows(i, tm, s_lat, modb_ref, modc_ref, idx):
    row = i * tm + lax.broadcasted_iota(jnp.int32, (tm, 1), 0)
    is_ctx = row >= s_lat
    return [jnp.where(is_ctx, modc_ref[k:k + 1, :], modb_ref[k:k + 1, :]) for k in idx]


def _modulated_rms(x, shift, scale):
    ms = jnp.mean(x * x, axis=-1, keepdims=True)
    return (x * lax.rsqrt(ms + EPS)) * (1.0 + scale) + shift


def _store_modulated_rms(h_scr, x_ref, i, tm, s_lat, modb_ref, modc_ref, k_shift, k_scale):
    @pl.when((i + 1) * tm > s_lat)
    def _():
        shift, scale = _mod_rows(i, tm, s_lat, modb_ref, modc_ref, (k_shift, k_scale))
        h_scr[...] = _modulated_rms(x_ref[...], shift, scale).astype(BF16)

    @pl.when((i + 1) * tm <= s_lat)
    def _():
        h_scr[...] = _modulated_rms(x_ref[...], modb_ref[k_shift:k_shift + 1, :],
                                    modb_ref[k_scale:k_scale + 1, :]).astype(BF16)


def _inproj_kernel(x_ref, modb_ref, modc_ref, w_ref, g_ref, cos_ref, sin_ref, seg_ref,
                   oqkv_ref, omid_ref, ogate_ref, h_scr, *, tm, s_lat):
    i = pl.program_id(1)
    _store_modulated_rms(h_scr, x_ref, i, tm, s_lat, modb_ref, modc_ref, 0, 1)

    def proj(c0, width):
        return jnp.dot(h_scr[...], w_ref[:, c0:c0 + width], preferred_element_type=F32)

    acc = proj(0, QKV_WIDTH)
    qk = acc[:, :QK_WIDTH]
    ms = jnp.dot((qk * qk).astype(BF16), seg_ref[...], preferred_element_type=F32)
    y = qk * lax.rsqrt(ms + EPS) * g_ref[...]
    cos = cos_ref[...]
    sin = sin_ref[...]
    lane = lax.broadcasted_iota(jnp.int32, (tm, LANES), 1)
    first = (lane % AXIS_DIM) < (AXIS_DIM // 2)
    for s in range(QK_WIDTH // LANES):
        ys = y[:, s * LANES:(s + 1) * LANES]
        partner = jnp.where(first, pltpu.roll(ys, LANES - AXIS_DIM // 2, 1), pltpu.roll(ys, AXIS_DIM // 2, 1))
        oqkv_ref[:, s * LANES:(s + 1) * LANES] = (ys * cos + partner * sin).astype(BF16)
    oqkv_ref[:, QK_WIDTH:] = acc[:, QK_WIDTH:].astype(BF16)

    acc = proj(OFF_CONF, 2 * CONF_WIDTH)
    omid_ref[:, M_GLU:M_GLU + CONF_WIDTH] = (acc[:, :CONF_WIDTH] * _sigmoid(acc[:, CONF_WIDTH:])).astype(BF16)
    acc = proj(OFF_SC, 3 * SC_WIDTH)
    omid_ref[:, M_BG:M_BG + SC_WIDTH] = acc[:, :SC_WIDTH].astype(BF16)
    omid_ref[:, M_CX:M_CX + SC_WIDTH] = (acc[:, SC_WIDTH:2 * SC_WIDTH] * acc[:, 2 * SC_WIDTH:]).astype(BF16)
    for m in range(N_BRANCHES):
        acc = proj(OFF_GATE + m * D_MODEL, D_MODEL)
        ogate_ref[:, m * D_MODEL:(m + 1) * D_MODEL] = _sigmoid(acc).astype(BF16)


def _in_proj(xs, mod_l, w, gvec, cos_t, sin_t, seg, *, s_lat, tm):
    b, t, d = xs.shape
    nb = mod_l.shape[0] - 1
    kern = functools.partial(_inproj_kernel, tm=tm, s_lat=s_lat)
    tok = lambda width: pl.BlockSpec((None, tm, width), lambda bi, i: (bi, i, 0))
    return pl.pallas_call(
        kern,
        grid=(b, t // tm),
        in_specs=[
            tok(d),
            pl.BlockSpec((None, N_MOD, d), lambda bi, i: (bi, 0, 0)),
            pl.BlockSpec((None, N_MOD, d), lambda bi, i: (nb, 0, 0)),
            _resident((d, IN_WIDTH)),
            _resident((1, QK_WIDTH)),
            pl.BlockSpec((tm, LANES), lambda bi, i: (i, 0)),
            pl.BlockSpec((tm, LANES), lambda bi, i: (i, 0)),
            _resident((QK_WIDTH, QK_WIDTH)),
        ],
        out_specs=[tok(QKV_WIDTH), tok(MID_WIDTH), tok(GATE_WIDTH)],
        out_shape=[
            jax.ShapeDtypeStruct((b, t, QKV_WIDTH), BF16),
            jax.ShapeDtypeStruct((b, t, MID_WIDTH), BF16),
            jax.ShapeDtypeStruct((b, t, GATE_WIDTH), BF16),
        ],
        scratch_shapes=[pltpu.VMEM((tm, d), BF16)],
        compiler_params=_cparams(("arbitrary", "arbitrary")),
        name="in_proj",
    )(xs, mod_l, mod_l, w, gvec, cos_t, sin_t, seg)


def _attn_kernel(q_ref, k_ref, v_ref, o_ref, kd_scr, ve_scr, *, tq, nk):
    i = pl.program_id(1)

    @pl.when(i == 0)
    def _():
        lo = lax.broadcasted_iota(jnp.int32, (nk, LANES), 1) < HEAD_DIM
        for src, dst in ((k_ref, kd_scr), (v_ref, ve_scr)):
            x = src[...].astype(F32)
            xs = pltpu.roll(x, HEAD_DIM, 1)
            dst[0, :, 0:LANES] = jnp.where(lo, x, xs).astype(BF16)
            dst[1, :, 0:LANES] = jnp.where(lo, xs, x).astype(BF16)
        for g in range(N_KV_HEADS):
            ve_scr[g, :, LANES:] = jnp.ones((nk, LANES), BF16)

    lo = lax.broadcasted_iota(jnp.int32, (ATTN_ROWS, LANES), 1) < HEAD_DIM
    for r0 in range(0, tq, ATTN_ROWS):
        for p in range(ATTN_WIDTH // LANES):
            g = (p * LANES // HEAD_DIM) // Q_GROUP
            qt = q_ref[r0:r0 + ATTN_ROWS, p * LANES:(p + 1) * LANES]
            outs = []
            for mask in (lo, jnp.logical_not(lo)):
                qm = jnp.where(mask, qt, jnp.zeros_like(qt))
                s = lax.dot_general(qm, kd_scr[g], (((1,), (1,)), ((), ())), preferred_element_type=F32)
                m = jnp.max(s, axis=-1, keepdims=True)
                e = jnp.exp2(s - m).astype(BF16)
                pv = jnp.dot(e, ve_scr[g], preferred_element_type=F32)
                outs.append(pv[:, :LANES] / pv[:, LANES:])
            o_ref[r0:r0 + ATTN_ROWS, p * LANES:(p + 1) * LANES] = jnp.where(lo, outs[0], outs[1]).astype(BF16)


def _attention(qkv, *, q_row0, n_q, tq, k_row0, n_k):
    b = qkv.shape[0]
    kern = functools.partial(_attn_kernel, tq=tq, nk=n_k)
    qb0, kb = q_row0 // tq, k_row0 // n_k
    return pl.pallas_call(
        kern,
        grid=(b, n_q // tq),
        in_specs=[
            pl.BlockSpec((None, tq, ATTN_WIDTH), lambda bi, i: (bi, qb0 + i, 0)),
            pl.BlockSpec((None, n_k, KV_WIDTH), lambda bi, i: (bi, kb, ATTN_WIDTH // KV_WIDTH)),
            pl.BlockSpec((None, n_k, KV_WIDTH), lambda bi, i: (bi, kb, QK_WIDTH // KV_WIDTH)),
        ],
        out_specs=pl.BlockSpec((None, tq, ATTN_WIDTH), lambda bi, i: (bi, i, 0)),
        out_shape=jax.ShapeDtypeStruct((b, n_q, ATTN_WIDTH), BF16),
        scratch_shapes=[pltpu.VMEM((N_KV_HEADS, n_k, LANES), BF16),
                        pltpu.VMEM((N_KV_HEADS, n_k, 2 * LANES), BF16)],
        compiler_params=_cparams(("arbitrary", "arbitrary")),
        name="attention",
    )(qkv, qkv, qkv)


def _merge_kernel(x_ref, m_ref, mp_ref, mn_ref, gt_ref, al_ref, ac_ref, modb_ref, modc_ref,
                  wao_ref, wco_ref, wso_ref, wmix_ref,
                  cw_ref, cb_ref, lg_ref, lb_ref, sw_ref,
                  o_ref, hp_scr, up_scr, hc_scr, sh_scr, *, tm, s_lat, t):
    i = pl.program_id(1)
    n_lat = s_lat // tm
    n_all = t // tm
    prev_ok = jnp.logical_and(i != 0, i != n_lat)
    next_ok = jnp.logical_and(i != n_lat - 1, i != n_all - 1)

    for scr, c0 in ((hp_scr, M_GLU), (up_scr, M_CX)):
        scr[0:HALO, :] = jnp.where(prev_ok, mp_ref[:, c0:c0 + CONF_WIDTH].astype(F32), 0.0)
        scr[HALO:HALO + tm, :] = m_ref[:, c0:c0 + CONF_WIDTH].astype(F32)
        scr[HALO + tm:, :] = jnp.where(next_ok, mn_ref[:, c0:c0 + CONF_WIDTH].astype(F32), 0.0)

    attn = jnp.where(i >= n_lat, ac_ref[...], al_ref[...])
    y_attn = jnp.dot(attn, wao_ref[...], preferred_element_type=F32)
    sacc = jnp.zeros((tm, SC_WIDTH), F32)
    for k in range(SC_KERNEL):
        off = HALO - SC_KERNEL // 2 + k
        sacc = sacc + sw_ref[k:k + 1, :] * up_scr[off:off + tm, :]
    bg = m_ref[:, M_BG:M_BG + SC_WIDTH].astype(F32)
    y_sc = jnp.dot((bg * sacc).astype(BF16), wso_ref[...], preferred_element_type=F32)

    rows = 128
    first_off = HALO - CONF_KERNEL // 2
    reach = (first_off + CONF_KERNEL - 1) // SUBLANES * SUBLANES
    for c0 in range(0, CONF_WIDTH, LANES):
        for r in range(1, SUBLANES):
            sh_scr[r - 1] = hp_scr[r:r + tm + reach, c0:c0 + LANES]
        for r0 in range(0, tm, rows):
            acc = jnp.zeros((rows, LANES), F32)
            for k in range(CONF_KERNEL):
                r = (first_off + k) % SUBLANES
                a8 = r0 + (first_off + k) // SUBLANES * SUBLANES
                src = hp_scr[a8:a8 + rows, c0:c0 + LANES] if r == 0 else sh_scr[r - 1, a8:a8 + rows, :]
                acc = acc + cw_ref[k:k + 1, c0:c0 + LANES] * src
            hc_scr[r0:r0 + rows, c0:c0 + LANES] = acc
    hc = hc_scr[...] + cb_ref[...]
    mu = jnp.mean(hc, axis=-1, keepdims=True)
    var = jnp.mean(jnp.square(hc - mu), axis=-1, keepdims=True)
    hn = (hc - mu) * lax.rsqrt(var + EPS) * lg_ref[...] + lb_ref[...]
    hn = hn * _sigmoid(hn)
    y_conf = jnp.dot(hn.astype(BF16), wco_ref[...], preferred_element_type=F32)

    def gate(m):
        return gt_ref[:, m * D_MODEL:(m + 1) * D_MODEL].astype(F32)

    merged = gate(0) * y_attn + gate(1) * y_conf + gate(2) * y_sc
    mixed = jnp.dot(merged.astype(BF16), wmix_ref[...], preferred_element_type=F32)
    (g_res,) = _mod_rows(i, tm, s_lat, modb_ref, modc_ref, (2,))
    o_ref[...] = x_ref[...] + g_res * mixed


def _merge(xs, mid, gates, attn_lat, attn_ctx, mod_l, wao, wco, wso, wmix, cw, cb, lg, lb, sw, *, s_lat, tm):
    b, t, d = xs.shape
    out_rows = s_lat if attn_ctx is None else t
    nb = mod_l.shape[0] - 1
    hb = tm // HALO
    n_hblk = t // HALO
    n_lat = s_lat // tm
    conv_in = CONF_WIDTH + SC_WIDTH
    kern = functools.partial(_merge_kernel, tm=tm, s_lat=s_lat, t=t)
    tok = lambda width: pl.BlockSpec((None, tm, width), lambda bi, i: (bi, i, 0))
    attn_second = attn_lat if attn_ctx is None else attn_ctx
    return pl.pallas_call(
        kern,
        grid=(b, out_rows // tm),
        in_specs=[
            tok(d),
            tok(MID_WIDTH),
            pl.BlockSpec((None, HALO, conv_in), lambda bi, i: (bi, jnp.maximum(i * hb - 1, 0), 0)),
            pl.BlockSpec((None, HALO, conv_in), lambda bi, i: (bi, jnp.minimum((i + 1) * hb, n_hblk - 1), 0)),
            tok(GATE_WIDTH),
            pl.BlockSpec((None, tm, ATTN_WIDTH), lambda bi, i: (bi, jnp.minimum(i, n_lat - 1), 0)),
            pl.BlockSpec((None, tm, ATTN_WIDTH), lambda bi, i: (bi, jnp.maximum(i - n_lat, 0), 0)),
            pl.BlockSpec((None, N_MOD, d), lambda bi, i: (bi, 0, 0)),
            pl.BlockSpec((None, N_MOD, d), lambda bi, i: (nb, 0, 0)),
            _resident((ATTN_WIDTH, d)),
            _resident((CONF_WIDTH, d)),
            _resident((SC_WIDTH, d)),
            _resident((d, d)),
            _resident((CONF_KERNEL, CONF_WIDTH)),
            _resident((1, CONF_WIDTH)),
            _resident((1, CONF_WIDTH)),
            _resident((1, CONF_WIDTH)),
            _resident((SC_KERNEL, SC_WIDTH)),
        ],
        out_specs=tok(d),
        out_shape=jax.ShapeDtypeStruct((b, out_rows, d), F32),
        scratch_shapes=[pltpu.VMEM((tm + 2 * HALO, CONF_WIDTH), F32),
                        pltpu.VMEM((tm + 2 * HALO, SC_WIDTH), F32),
                        pltpu.VMEM((tm, CONF_WIDTH), F32),
                        pltpu.VMEM((SUBLANES - 1, tm + 2 * HALO - SUBLANES, LANES), F32)],
        compiler_params=_cparams(("arbitrary", "arbitrary")),
        name="merge",
    )(xs, mid, mid, mid, gates, attn_lat, attn_second, mod_l, mod_l, wao, wco, wso, wmix, cw, cb, lg, lb, sw)


def _ffn_kernel(x_ref, modb_ref, modc_ref, wa_ref, wb_ref, wo_ref, o_ref, h_scr, u_scr, *, tm, s_lat):
    i = pl.program_id(1)
    _store_modulated_rms(h_scr, x_ref, i, tm, s_lat, modb_ref, modc_ref, 3, 4)
    for c0 in range(0, FFN_HIDDEN, FFN_TH):
        cols = slice(c0, min(c0 + FFN_TH, FFN_HIDDEN))
        a = jnp.dot(h_scr[...], wa_ref[:, cols], preferred_element_type=F32)
        bb = jnp.dot(h_scr[...], wb_ref[:, cols], preferred_element_type=F32)
        u_scr[:, cols] = (a * _sigmoid(a) * bb).astype(BF16)
    y = jnp.dot(u_scr[...], wo_ref[...], preferred_element_type=F32)
    (g_res,) = _mod_rows(i, tm, s_lat, modb_ref, modc_ref, (5,))
    o_ref[...] = x_ref[...] + g_res * y


def _ffn(xs, mod_l, w_in, w_out, *, s_lat, tm):
    b, t, d = xs.shape
    nb = mod_l.shape[0] - 1
    kern = functools.partial(_ffn_kernel, tm=tm, s_lat=s_lat)
    tok = pl.BlockSpec((None, tm, d), lambda bi, i: (bi, i, 0))
    return pl.pallas_call(
        kern,
        grid=(b, t // tm),
        in_specs=[
            tok,
            pl.BlockSpec((None, N_MOD, d), lambda bi, i: (bi, 0, 0)),
            pl.BlockSpec((None, N_MOD, d), lambda bi, i: (nb, 0, 0)),
            pl.BlockSpec((d, FFN_HIDDEN), lambda bi, i: (0, 0), pipeline_mode=pl.Buffered(1)),
            pl.BlockSpec((d, FFN_HIDDEN), lambda bi, i: (0, 1), pipeline_mode=pl.Buffered(1)),
            _resident((FFN_HIDDEN, d)),
        ],
        out_specs=tok,
        out_shape=jax.ShapeDtypeStruct((b, t, d), F32),
        scratch_shapes=[pltpu.VMEM((tm, d), BF16), pltpu.VMEM((tm, FFN_HIDDEN), BF16)],
        compiler_params=_cparams(("arbitrary", "arbitrary")),
        name="ffn",
    )(xs, mod_l, mod_l, w_in, w_in, w_out)


def _rope_tables(s, lc):
    pos = jnp.arange(s)
    r_ids = (pos // GRID_W).astype(F32)
    c_ids = (pos % GRID_W).astype(F32)
    freqs = ROPE_THETA ** (-jnp.arange(0, AXIS_DIM, 2, dtype=F32) / AXIS_DIM)
    ang_r = r_ids[:, None] * freqs
    ang_c = c_ids[:, None] * freqs
    cos_h = jnp.concatenate([jnp.cos(ang_r), jnp.cos(ang_r), jnp.cos(ang_c), jnp.cos(ang_c)], axis=1)
    sin_h = jnp.concatenate([-jnp.sin(ang_r), jnp.sin(ang_r), -jnp.sin(ang_c), jnp.sin(ang_c)], axis=1)
    reps = LANES // HEAD_DIM
    cos_t = jnp.concatenate([jnp.tile(cos_h, (1, reps)), jnp.ones((lc, LANES), F32)], axis=0)
    sin_t = jnp.concatenate([jnp.tile(sin_h, (1, reps)), jnp.zeros((lc, LANES), F32)], axis=0)
    return cos_t, sin_t


def kernel(x, c, ctx, c_ctx, w_ada, b_ada, w_in, q_norm, k_norm, w_attn_o, conf_dw_w, conf_dw_b,
           conf_ln_g, conf_ln_b, w_conf_out, sc_dw_w, w_sc_out, w_mix_out, w_ffn_in, w_ffn_out):
    b, s, d = x.shape
    lc = ctx.shape[1]
    n_layers = w_ada.shape[0]
    assert d == D_MODEL and lc == ATTN_ROWS and s % 1024 == 0 and (s + lc) % 768 == 0

    xs = jnp.concatenate([x, ctx], axis=1)

    mp = -(-(b + 1) // SUBLANES) * SUBLANES
    cc = jnp.concatenate([c, jnp.zeros((mp - b - 1, d), F32), c_ctx[None, :]], axis=0)
    mod = _modulation(cc, w_ada, b_ada).reshape(n_layers, mp, N_MOD, d)

    cos_t, sin_t = _rope_tables(s, lc)
    head_of_lane = jnp.arange(QK_WIDTH) // HEAD_DIM
    seg = (head_of_lane[:, None] == head_of_lane[None, :]).astype(BF16) * (1.0 / HEAD_DIM)
    w_in, w_attn_o, w_conf_out, w_sc_out, w_mix_out, w_ffn_in, w_ffn_out = (
        w.astype(BF16) for w in (w_in, w_attn_o, w_conf_out, w_sc_out, w_mix_out, w_ffn_in, w_ffn_out))

    for i in range(n_layers):
        gvec = jnp.concatenate([jnp.tile(q_norm[i] * (ATTN_SCALE * LOG2E), N_Q_HEADS),
                                jnp.tile(k_norm[i], N_KV_HEADS)])[None, :]
        last = i == n_layers - 1
        qkv, mid, gates = _in_proj(xs, mod[i], w_in[i], gvec, cos_t, sin_t, seg, s_lat=s, tm=768)
        attn_lat = _attention(qkv, q_row0=0, n_q=s, tq=1024, k_row0=0, n_k=s + lc)
        attn_ctx = None if last else _attention(qkv, q_row0=s, n_q=lc, tq=lc, k_row0=s, n_k=lc)
        xs = _merge(xs, mid, gates, attn_lat, attn_ctx, mod[i], w_attn_o[i], w_conf_out[i], w_sc_out[i], w_mix_out[i],
                    conf_dw_w[i], conf_dw_b[i][None, :], conf_ln_g[i][None, :], conf_ln_b[i][None, :],
                    sc_dw_w[i], s_lat=s, tm=256)
        xs = _ffn(xs, mod[i], w_ffn_in[i], w_ffn_out[i], s_lat=s, tm=512 if last else 768)
    return xs
```

```python
import functools

import jax
import jax.numpy as jnp
from jax import lax
from jax.experimental import pallas as pl
from jax.experimental.pallas import tpu as pltpu

F32 = jnp.float32
BF16 = jnp.bfloat16

D_MODEL = 1024
HEAD_DIM = 64
N_Q_HEADS = 8
N_KV_HEADS = 2
Q_GROUP = N_Q_HEADS // N_KV_HEADS
ATTN_WIDTH = N_Q_HEADS * HEAD_DIM
KV_WIDTH = N_KV_HEADS * HEAD_DIM
AXIS_DIM = HEAD_DIM // 2
ROPE_THETA = 10000.0
GRID_W = 64
ATTN_SCALE = HEAD_DIM ** -0.5
CONF_WIDTH = D_MODEL // 2
CONF_KERNEL = 31
SC_WIDTH = D_MODEL // 2
SC_KERNEL = 3
N_BRANCHES = 3
N_MOD = 6
FFN_HIDDEN = 2816
EPS = 1e-6

OFF_K = ATTN_WIDTH
OFF_V = OFF_K + KV_WIDTH
OFF_CONF = OFF_V + KV_WIDTH
OFF_SC = OFF_CONF + 2 * CONF_WIDTH
OFF_GATE = OFF_SC + 3 * SC_WIDTH
IN_WIDTH = OFF_GATE + N_BRANCHES * D_MODEL

LANES = 128
SUBLANES = 8
MXU_N = 256
QK_WIDTH = ATTN_WIDTH + KV_WIDTH
QKV_WIDTH = QK_WIDTH + KV_WIDTH
MID_WIDTH = CONF_WIDTH + 2 * SC_WIDTH
M_GLU = 0
M_CX = CONF_WIDTH
M_BG = CONF_WIDTH + SC_WIDTH
GATE_WIDTH = N_BRANCHES * D_MODEL
HALO = 16
FFN_TH = 2 * MXU_N
ATTN_ROWS = 256
LOG2E = 1.4426950408889634
VMEM_LIMIT = 56 * 1024 * 1024


def _cparams(sem):
    return pltpu.CompilerParams(dimension_semantics=sem, vmem_limit_bytes=VMEM_LIMIT)


def _resident(shape):
    return pl.BlockSpec(shape, lambda *_: (0,) * len(shape), pipeline_mode=pl.Buffered(1))


def _sigmoid(x):
    return 1.0 / (1.0 + jnp.exp(-x))


def _mod_kernel(c_ref, w_ref, b_ref, o_ref):
    a = c_ref[...]
    a = (a * _sigmoid(a)).astype(BF16)
    o_ref[...] = jnp.dot(a, w_ref[...].astype(BF16), preferred_element_type=F32) + b_ref[...]


def _modulation(cc, w_ada, b_ada):
    n_layers, d, n = w_ada.shape
    mp = cc.shape[0]
    tn = 1536
    return pl.pallas_call(
        _mod_kernel,
        grid=(n_layers, n // tn),
        in_specs=[
            pl.BlockSpec((mp, d), lambda l, j: (0, 0)),
            pl.BlockSpec((None, d, tn), lambda l, j: (l, 0, j)),
            pl.BlockSpec((None, 1, tn), lambda l, j: (l, 0, j)),
        ],
        out_specs=pl.BlockSpec((None, mp, tn), lambda l, j: (l, 0, j)),
        out_shape=jax.ShapeDtypeStruct((n_layers, mp, n), F32),
        compiler_params=_cparams(("arbitrary", "arbitrary")),
        name="adaln_mod",
    )(cc, w_ada, b_ada.reshape(n_layers, 1, n))


def _mod_rows(i, tm, s_lat, modb_ref, modc_ref, idx):
    row = i * tm + lax.broadcasted_iota(jnp.int32, (tm, 1), 0)
    is_ctx = row >= s_lat
    return [jnp.where(is_ctx, modc_ref[k:k + 1, :], modb_ref[k:k + 1, :]) for k in idx]


def _modulated_rms(x, shift, scale):
    ms = jnp.mean(x * x, axis=-1, keepdims=True)
    return (x * lax.rsqrt(ms + EPS)) * (1.0 + scale) + shift


def _store_modulated_rms(h_scr, x_ref, i, tm, s_lat, modb_ref, modc_ref, k_shift, k_scale):
    @pl.when((i + 1) * tm > s_lat)
    def _():
        shift, scale = _mod_rows(i, tm, s_lat, modb_ref, modc_ref, (k_shift, k_scale))
        h_scr[...] = _modulated_rms(x_ref[...], shift, scale).astype(BF16)

    @pl.when((i + 1) * tm <= s_lat)
    def _():
        h_scr[...] = _modulated_rms(x_ref[...], modb_ref[k_shift:k_shift + 1, :],
                                    modb_ref[k_scale:k_scale + 1, :]).astype(BF16)


def _inproj_kernel(x_ref, modb_ref, modc_ref, w_ref, g_ref, cos_ref, sin_ref, seg_ref,
                   oqkv_ref, omid_ref, ogate_ref, h_scr, *, tm, s_lat):
    i = pl.program_id(1)
    _store_modulated_rms(h_scr, x_ref, i, tm, s_lat, modb_ref, modc_ref, 0, 1)

    def proj(c0, width):
        return jnp.dot(h_scr[...], w_ref[:, c0:c0 + width], preferred_element_type=F32)

    acc = proj(0, QKV_WIDTH)
    cos = cos_ref[...]
    sin = sin_ref[...]
    lane = lax.broadcasted_iota(jnp.int32, (tm, LANES), 1)
    first = (lane % AXIS_DIM) < (AXIS_DIM // 2)
    for s in range(QK_WIDTH // LANES):
        cols = slice(s * LANES, (s + 1) * LANES)
        qk = acc[:, cols]
        ms = jnp.dot((qk * qk).astype(BF16), seg_ref[...], preferred_element_type=F32)
        ys = qk * lax.rsqrt(ms + EPS) * g_ref[:, cols]
        partner = jnp.where(first, pltpu.roll(ys, LANES - AXIS_DIM // 2, 1), pltpu.roll(ys, AXIS_DIM // 2, 1))
        oqkv_ref[:, s * LANES:(s + 1) * LANES] = (ys * cos + partner * sin).astype(BF16)
    oqkv_ref[:, QK_WIDTH:] = acc[:, QK_WIDTH:].astype(BF16)

    acc = proj(OFF_CONF, 2 * CONF_WIDTH)
    omid_ref[:, M_GLU:M_GLU + CONF_WIDTH] = (acc[:, :CONF_WIDTH] * _sigmoid(acc[:, CONF_WIDTH:])).astype(BF16)
    acc = proj(OFF_SC, 3 * SC_WIDTH)
    omid_ref[:, M_BG:M_BG + SC_WIDTH] = acc[:, :SC_WIDTH].astype(BF16)
    omid_ref[:, M_CX:M_CX + SC_WIDTH] = (acc[:, SC_WIDTH:2 * SC_WIDTH] * acc[:, 2 * SC_WIDTH:]).astype(BF16)
    for m in range(N_BRANCHES):
        acc = proj(OFF_GATE + m * D_MODEL, D_MODEL)
        ogate_ref[:, m * D_MODEL:(m + 1) * D_MODEL] = _sigmoid(acc).astype(BF16)


def _in_proj(xs, mod_l, w, gvec, cos_t, sin_t, seg, *, s_lat, tm):
    b, t, d = xs.shape
    nb = mod_l.shape[0] - 1
    kern = functools.partial(_inproj_kernel, tm=tm, s_lat=s_lat)
    tok = lambda width: pl.BlockSpec((None, tm, width), lambda bi, i: (bi, i, 0))
    return pl.pallas_call(
        kern,
        grid=(b, t // tm),
        in_specs=[
            tok(d),
            pl.BlockSpec((None, N_MOD, d), lambda bi, i: (bi, 0, 0)),
            pl.BlockSpec((None, N_MOD, d), lambda bi, i: (nb, 0, 0)),
            _resident((d, IN_WIDTH)),
            _resident((1, QK_WIDTH)),
            pl.BlockSpec((tm, LANES), lambda bi, i: (i, 0)),
            pl.BlockSpec((tm, LANES), lambda bi, i: (i, 0)),
            _resident((LANES, LANES)),
        ],
        out_specs=[tok(QKV_WIDTH), tok(MID_WIDTH), tok(GATE_WIDTH)],
        out_shape=[
            jax.ShapeDtypeStruct((b, t, QKV_WIDTH), BF16),
            jax.ShapeDtypeStruct((b, t, MID_WIDTH), BF16),
            jax.ShapeDtypeStruct((b, t, GATE_WIDTH), BF16),
        ],
        scratch_shapes=[pltpu.VMEM((tm, d), BF16)],
        compiler_params=_cparams(("arbitrary", "arbitrary")),
        name="in_proj",
    )(xs, mod_l, mod_l, w, gvec, cos_t, sin_t, seg)


def _attn_kernel(q_ref, k_ref, v_ref, o_ref, kd_scr, ve_scr, *, tq, nk):
    i = pl.program_id(1)

    @pl.when(i == 0)
    def _():
        lo = lax.broadcasted_iota(jnp.int32, (nk, LANES), 1) < HEAD_DIM
        for src, dst in ((k_ref, kd_scr), (v_ref, ve_scr)):
            x = src[...].astype(F32)
            xs = pltpu.roll(x, HEAD_DIM, 1)
            dst[0, :, 0:LANES] = jnp.where(lo, x, xs).astype(BF16)
            dst[1, :, 0:LANES] = jnp.where(lo, xs, x).astype(BF16)
        for g in range(N_KV_HEADS):
            ve_scr[g, :, LANES:] = jnp.ones((nk, LANES), BF16)

    lo = lax.broadcasted_iota(jnp.int32, (ATTN_ROWS, LANES), 1) < HEAD_DIM
    for r0 in range(0, tq, ATTN_ROWS):
        for p in range(ATTN_WIDTH // LANES):
            g = (p * LANES // HEAD_DIM) // Q_GROUP
            qt = q_ref[r0:r0 + ATTN_ROWS, p * LANES:(p + 1) * LANES]
            outs = []
            for mask in (lo, jnp.logical_not(lo)):
                qm = jnp.where(mask, qt, jnp.zeros_like(qt))
                s = lax.dot_general(qm, kd_scr[g], (((1,), (1,)), ((), ())), preferred_element_type=F32)
                m = jnp.max(s, axis=-1, keepdims=True)
                e = jnp.exp2(s - m).astype(BF16)
                pv = jnp.dot(e, ve_scr[g], preferred_element_type=F32)
                outs.append(pv[:, :LANES] / pv[:, LANES:])
            o_ref[r0:r0 + ATTN_ROWS, p * LANES:(p + 1) * LANES] = jnp.where(lo, outs[0], outs[1]).astype(BF16)


def _attn_kernel_into(q_ref, k_ref, v_ref, prev_ref, o_ref, kd_scr, ve_scr, *, tq, nk):
    del prev_ref
    _attn_kernel(q_ref, k_ref, v_ref, o_ref, kd_scr, ve_scr, tq=tq, nk=nk)


def _attention(qkv, out_prev, *, q_row0, n_q, tq, k_row0, n_k):
    b, t, _ = qkv.shape
    qb0, kb = q_row0 // tq, k_row0 // n_k
    in_specs = [
        pl.BlockSpec((None, tq, ATTN_WIDTH), lambda bi, i: (bi, qb0 + i, 0)),
        pl.BlockSpec((None, n_k, KV_WIDTH), lambda bi, i: (bi, kb, ATTN_WIDTH // KV_WIDTH)),
        pl.BlockSpec((None, n_k, KV_WIDTH), lambda bi, i: (bi, kb, QK_WIDTH // KV_WIDTH)),
    ]
    args = [qkv, qkv, qkv]
    if out_prev is None:
        kern, aliases = functools.partial(_attn_kernel, tq=tq, nk=n_k), {}
    else:
        kern, aliases = functools.partial(_attn_kernel_into, tq=tq, nk=n_k), {len(args): 0}
        in_specs.append(pl.BlockSpec(memory_space=pl.ANY))
        args.append(out_prev)
    return pl.pallas_call(
        kern,
        grid=(b, n_q // tq),
        in_specs=in_specs,
        out_specs=pl.BlockSpec((None, tq, ATTN_WIDTH), lambda bi, i: (bi, qb0 + i, 0)),
        out_shape=jax.ShapeDtypeStruct((b, t, ATTN_WIDTH), BF16),
        input_output_aliases=aliases,
        scratch_shapes=[pltpu.VMEM((N_KV_HEADS, n_k, LANES), BF16),
                        pltpu.VMEM((N_KV_HEADS, n_k, 2 * LANES), BF16)],
        compiler_params=_cparams(("arbitrary", "arbitrary")),
        name="attention",
    )(*args)


def _merge_kernel(x_ref, m_ref, mp_ref, mn_ref, gt_ref, a_ref, modb_ref, modc_ref,
                  wao_ref, wco_ref, wso_ref, wmix_ref,
                  cw_ref, cb_ref, lg_ref, lb_ref, sw_ref,
                  o_ref, hp_scr, up_scr, hc_scr, sh_scr, *, tm, s_lat, t):
    i = pl.program_id(1)
    sub = ATTN_ROWS
    n_sub = tm // sub
    n_lat = s_lat // sub
    n_all = t // sub
    rows = 128
    first_off = HALO - CONF_KERNEL // 2
    reach = (first_off + CONF_KERNEL - 1) // SUBLANES * SUBLANES

    for u in range(n_sub):
        j = i * n_sub + u
        r_lo, r_hi = u * sub, (u + 1) * sub
        main = slice(r_lo, r_hi)
        prev_ok = jnp.logical_and(j != 0, j != n_lat)
        next_ok = jnp.logical_and(j != n_lat - 1, j != n_all - 1)
        hp, up, hc, sh = hp_scr.at[u], up_scr.at[u], hc_scr.at[u], sh_scr.at[u]

        for scr, c0 in ((hp, M_GLU), (up, M_CX)):
            cols = slice(c0, c0 + CONF_WIDTH)
            prev = mp_ref[:, cols] if u == 0 else m_ref[r_lo - HALO:r_lo, cols]
            nxt = mn_ref[:, cols] if u == n_sub - 1 else m_ref[r_hi:r_hi + HALO, cols]
            scr[0:HALO, :] = jnp.where(prev_ok, prev.astype(F32), 0.0)
            scr[HALO:HALO + sub, :] = m_ref[main, cols].astype(F32)
            scr[HALO + sub:, :] = jnp.where(next_ok, nxt.astype(F32), 0.0)

        y_attn = jnp.dot(a_ref[main, :], wao_ref[...], preferred_element_type=F32)
        sacc = jnp.zeros((sub, SC_WIDTH), F32)
        for k in range(SC_KERNEL):
            off = HALO - SC_KERNEL // 2 + k
            sacc = sacc + sw_ref[k:k + 1, :] * up[off:off + sub, :]
        bg = m_ref[main, M_BG:M_BG + SC_WIDTH].astype(F32)
        y_sc = jnp.dot((bg * sacc).astype(BF16), wso_ref[...], preferred_element_type=F32)

        for c0 in range(0, CONF_WIDTH, LANES):
            for r in range(1, SUBLANES):
                sh[r - 1] = hp[r:r + sub + reach, c0:c0 + LANES]
            for r0 in range(0, sub, rows):
                acc = jnp.zeros((rows, LANES), F32)
                for k in range(CONF_KERNEL):
                    r = (first_off + k) % SUBLANES
                    a8 = r0 + (first_off + k) // SUBLANES * SUBLANES
                    src = hp[a8:a8 + rows, c0:c0 + LANES] if r == 0 else sh[r - 1, a8:a8 + rows, :]
                    acc = acc + cw_ref[k:k + 1, c0:c0 + LANES] * src
                hc[r0:r0 + rows, c0:c0 + LANES] = acc
        hcv = hc[...] + cb_ref[...]
        mu = jnp.mean(hcv, axis=-1, keepdims=True)
        var = jnp.mean(jnp.square(hcv - mu), axis=-1, keepdims=True)
        hn = (hcv - mu) * lax.rsqrt(var + EPS) * lg_ref[...] + lb_ref[...]
        hn = hn * _sigmoid(hn)
        y_conf = jnp.dot(hn.astype(BF16), wco_ref[...], preferred_element_type=F32)

        def gate(m):
            return gt_ref[main, m * D_MODEL:(m + 1) * D_MODEL].astype(F32)

        merged = gate(0) * y_attn + gate(1) * y_conf + gate(2) * y_sc
        mixed = jnp.dot(merged.astype(BF16), wmix_ref[...], preferred_element_type=F32)
        g_res = jnp.where(j >= n_lat, modc_ref[2:3, :], modb_ref[2:3, :])
        o_ref[main, :] = x_ref[main, :] + g_res * mixed


def _merge(xs, mid, gates, attn, mod_l, wao, wco, wso, wmix, cw, cb, lg, lb, sw, *, s_lat, tm, out_rows):
    b, t, d = xs.shape
    nb = mod_l.shape[0] - 1
    hb = tm // HALO
    n_hblk = t // HALO
    conv_in = CONF_WIDTH + SC_WIDTH
    kern = functools.partial(_merge_kernel, tm=tm, s_lat=s_lat, t=t)
    sub, n_sub = ATTN_ROWS, tm // ATTN_ROWS
    tok = lambda width: pl.BlockSpec((None, tm, width), lambda bi, i: (bi, i, 0))
    return pl.pallas_call(
        kern,
        grid=(b, out_rows // tm),
        in_specs=[
            tok(d),
            tok(MID_WIDTH),
            pl.BlockSpec((None, HALO, conv_in), lambda bi, i: (bi, jnp.maximum(i * hb - 1, 0), 0)),
            pl.BlockSpec((None, HALO, conv_in), lambda bi, i: (bi, jnp.minimum((i + 1) * hb, n_hblk - 1), 0)),
            tok(GATE_WIDTH),
            tok(ATTN_WIDTH),
            pl.BlockSpec((None, N_MOD, d), lambda bi, i: (bi, 0, 0)),
            pl.BlockSpec((None, N_MOD, d), lambda bi, i: (nb, 0, 0)),
            _resident((ATTN_WIDTH, d)),
            _resident((CONF_WIDTH, d)),
            _resident((SC_WIDTH, d)),
            _resident((d, d)),
            _resident((CONF_KERNEL, CONF_WIDTH)),
            _resident((1, CONF_WIDTH)),
            _resident((1, CONF_WIDTH)),
            _resident((1, CONF_WIDTH)),
            _resident((SC_KERNEL, SC_WIDTH)),
        ],
        out_specs=tok(d),
        out_shape=jax.ShapeDtypeStruct((b, out_rows, d), F32),
        scratch_shapes=[pltpu.VMEM((n_sub, sub + 2 * HALO, CONF_WIDTH), F32),
                        pltpu.VMEM((n_sub, sub + 2 * HALO, SC_WIDTH), F32),
                        pltpu.VMEM((n_sub, sub, CONF_WIDTH), F32),
                        pltpu.VMEM((n_sub, SUBLANES - 1, sub + 2 * HALO - SUBLANES, LANES), F32)],
        compiler_params=_cparams(("arbitrary", "arbitrary")),
        name="merge",
    )(xs, mid, mid, mid, gates, attn, mod_l, mod_l, wao, wco, wso, wmix, cw, cb, lg, lb, sw)


def _ffn_kernel(x_ref, modb_ref, modc_ref, wa_ref, wb_ref, wo_ref, o_ref, h_scr, u_scr, *, tm, s_lat):
    i = pl.program_id(1)
    _store_modulated_rms(h_scr, x_ref, i, tm, s_lat, modb_ref, modc_ref, 3, 4)
    for c0 in range(0, FFN_HIDDEN, FFN_TH):
        cols = slice(c0, min(c0 + FFN_TH, FFN_HIDDEN))
        a = jnp.dot(h_scr[...], wa_ref[:, cols], preferred_element_type=F32)
        bb = jnp.dot(h_scr[...], wb_ref[:, cols], preferred_element_type=F32)
        u_scr[:, cols] = (a * _sigmoid(a) * bb).astype(BF16)
    y = jnp.dot(u_scr[...], wo_ref[...], preferred_element_type=F32)
    (g_res,) = _mod_rows(i, tm, s_lat, modb_ref, modc_ref, (5,))
    o_ref[...] = x_ref[...] + g_res * y


def _ffn(xs, mod_l, w_in, w_out, *, s_lat, tm):
    b, t, d = xs.shape
    nb = mod_l.shape[0] - 1
    kern = functools.partial(_ffn_kernel, tm=tm, s_lat=s_lat)
    tok = pl.BlockSpec((None, tm, d), lambda bi, i: (bi, i, 0))
    return pl.pallas_call(
        kern,
        grid=(b, t // tm),
        in_specs=[
            tok,
            pl.BlockSpec((None, N_MOD, d), lambda bi, i: (bi, 0, 0)),
            pl.BlockSpec((None, N_MOD, d), lambda bi, i: (nb, 0, 0)),
            pl.BlockSpec((d, FFN_HIDDEN), lambda bi, i: (0, 0), pipeline_mode=pl.Buffered(1)),
            pl.BlockSpec((d, FFN_HIDDEN), lambda bi, i: (0, 1), pipeline_mode=pl.Buffered(1)),
            _resident((FFN_HIDDEN, d)),
        ],
        out_specs=tok,
        out_shape=jax.ShapeDtypeStruct((b, t, d), F32),
        scratch_shapes=[pltpu.VMEM((tm, d), BF16), pltpu.VMEM((tm, FFN_HIDDEN), BF16)],
        compiler_params=_cparams(("arbitrary", "arbitrary")),
        name="ffn",
    )(xs, mod_l, mod_l, w_in, w_in, w_out)


def _rope_tables(s, lc):
    pos = jnp.arange(s)
    r_ids = (pos // GRID_W).astype(F32)
    c_ids = (pos % GRID_W).astype(F32)
    freqs = ROPE_THETA ** (-jnp.arange(0, AXIS_DIM, 2, dtype=F32) / AXIS_DIM)
    ang_r = r_ids[:, None] * freqs
    ang_c = c_ids[:, None] * freqs
    cos_h = jnp.concatenate([jnp.cos(ang_r), jnp.cos(ang_r), jnp.cos(ang_c), jnp.cos(ang_c)], axis=1)
    sin_h = jnp.concatenate([-jnp.sin(ang_r), jnp.sin(ang_r), -jnp.sin(ang_c), jnp.sin(ang_c)], axis=1)
    reps = LANES // HEAD_DIM
    cos_t = jnp.concatenate([jnp.tile(cos_h, (1, reps)), jnp.ones((lc, LANES), F32)], axis=0)
    sin_t = jnp.concatenate([jnp.tile(sin_h, (1, reps)), jnp.zeros((lc, LANES), F32)], axis=0)
    return cos_t, sin_t


def kernel(x, c, ctx, c_ctx, w_ada, b_ada, w_in, q_norm, k_norm, w_attn_o, conf_dw_w, conf_dw_b,
           conf_ln_g, conf_ln_b, w_conf_out, sc_dw_w, w_sc_out, w_mix_out, w_ffn_in, w_ffn_out):
    b, s, d = x.shape
    lc = ctx.shape[1]
    n_layers = w_ada.shape[0]
    assert d == D_MODEL and lc == ATTN_ROWS and s % 1024 == 0 and (s + lc) % 768 == 0

    xs = jnp.concatenate([x, ctx], axis=1)

    mp = -(-(b + 1) // SUBLANES) * SUBLANES
    cc = jnp.concatenate([c, jnp.zeros((mp - b - 1, d), F32), c_ctx[None, :]], axis=0)
    mod = _modulation(cc, w_ada, b_ada).reshape(n_layers, mp, N_MOD, d)

    cos_t, sin_t = _rope_tables(s, lc)
    head_of_lane = jnp.arange(LANES) // HEAD_DIM
    seg = (head_of_lane[:, None] == head_of_lane[None, :]).astype(BF16) * (1.0 / HEAD_DIM)
    w_in, w_attn_o, w_conf_out, w_sc_out, w_mix_out, w_ffn_in, w_ffn_out = (
        w.astype(BF16) for w in (w_in, w_attn_o, w_conf_out, w_sc_out, w_mix_out, w_ffn_in, w_ffn_out))

    for i in range(n_layers):
        gvec = jnp.concatenate([jnp.tile(q_norm[i] * (ATTN_SCALE * LOG2E), N_Q_HEADS),
                                jnp.tile(k_norm[i], N_KV_HEADS)])[None, :]
        last = i == n_layers - 1
        qkv, mid, gates = _in_proj(xs, mod[i], w_in[i], gvec, cos_t, sin_t, seg, s_lat=s, tm=768)
        attn = _attention(qkv, None, q_row0=0, n_q=s, tq=1024, k_row0=0, n_k=s + lc)
        if not last:
            attn = _attention(qkv, attn, q_row0=s, n_q=lc, tq=lc, k_row0=s, n_k=lc)
        xs = _merge(xs, mid, gates, attn, mod[i], w_attn_o[i], w_conf_out[i], w_sc_out[i], w_mix_out[i],
                    conf_dw_w[i], conf_dw_b[i][None, :], conf_ln_g[i][None, :], conf_ln_b[i][None, :],
                    sc_dw_w[i], s_lat=s, tm=512 if last else 768, out_rows=s if last else s + lc)
        xs = _ffn(xs, mod[i], w_ffn_in[i], w_ffn_out[i], s_lat=s, tm=512 if last else 768)
    return xs
```

```python
import functools

import jax
import jax.numpy as jnp
from jax import lax
from jax.experimental import pallas as pl
from jax.experimental.pallas import tpu as pltpu

F32 = jnp.float32
BF16 = jnp.bfloat16

D_MODEL = 1024
HEAD_DIM = 64
N_Q_HEADS = 8
N_KV_HEADS = 2
Q_GROUP = N_Q_HEADS // N_KV_HEADS
ATTN_WIDTH = N_Q_HEADS * HEAD_DIM
KV_WIDTH = N_KV_HEADS * HEAD_DIM
AXIS_DIM = HEAD_DIM // 2
ROPE_THETA = 10000.0
GRID_W = 64
ATTN_SCALE = HEAD_DIM ** -0.5
CONF_WIDTH = D_MODEL // 2
CONF_KERNEL = 31
SC_WIDTH = D_MODEL // 2
SC_KERNEL = 3
N_BRANCHES = 3
N_MOD = 6
FFN_HIDDEN = 2816
EPS = 1e-6

OFF_K = ATTN_WIDTH
OFF_V = OFF_K + KV_WIDTH
OFF_CONF = OFF_V + KV_WIDTH
OFF_SC = OFF_CONF + 2 * CONF_WIDTH
OFF_GATE = OFF_SC + 3 * SC_WIDTH
IN_WIDTH = OFF_GATE + N_BRANCHES * D_MODEL

LANES = 128
SUBLANES = 8
MXU_N = 256
QK_WIDTH = ATTN_WIDTH + KV_WIDTH
QKV_WIDTH = QK_WIDTH + KV_WIDTH
MID_WIDTH = CONF_WIDTH + 2 * SC_WIDTH
M_GLU = 0
M_CX = CONF_WIDTH
M_BG = CONF_WIDTH + SC_WIDTH
GATE_WIDTH = N_BRANCHES * D_MODEL
HALO = 16
FFN_TH = 2 * MXU_N
ATTN_ROWS = 256
LOG2E = 1.4426950408889634
VMEM_LIMIT = 56 * 1024 * 1024


def _cparams(sem):
    return pltpu.CompilerParams(dimension_semantics=sem, vmem_limit_bytes=VMEM_LIMIT)


def _resident(shape):
    return pl.BlockSpec(shape, lambda *_: (0,) * len(shape), pipeline_mode=pl.Buffered(1))


def _sigmoid(x):
    return 1.0 / (1.0 + jnp.exp(-x))


def _mod_kernel(c_ref, w_ref, b_ref, o_ref):
    a = c_ref[...]
    a = (a * _sigmoid(a)).astype(BF16)
    o_ref[...] = jnp.dot(a, w_ref[...].astype(BF16), preferred_element_type=F32) + b_ref[...]


def _modulation(cc, w_ada, b_ada):
    n_layers, d, n = w_ada.shape
    mp = cc.shape[0]
    tn = 1536
    return pl.pallas_call(
        _mod_kernel,
        grid=(n_layers, n // tn),
        in_specs=[
            pl.BlockSpec((mp, d), lambda l, j: (0, 0)),
            pl.BlockSpec((None, d, tn), lambda l, j: (l, 0, j)),
            pl.BlockSpec((None, 1, tn), lambda l, j: (l, 0, j)),
        ],
        out_specs=pl.BlockSpec((None, mp, tn), lambda l, j: (l, 0, j)),
        out_shape=jax.ShapeDtypeStruct((n_layers, mp, n), F32),
        compiler_params=_cparams(("arbitrary", "arbitrary")),
        name="adaln_mod",
    )(cc, w_ada, b_ada.reshape(n_layers, 1, n))


def _mod_rows(i, tm, s_lat, modb_ref, modc_ref, idx):
    row = i * tm + lax.broadcasted_iota(jnp.int32, (tm, 1), 0)
    is_ctx = row >= s_lat
    return [jnp.where(is_ctx, modc_ref[k:k + 1, :], modb_ref[k:k + 1, :]) for k in idx]


def _modulated_rms(x, shift, scale):
    ms = jnp.mean(x * x, axis=-1, keepdims=True)
    return (x * lax.rsqrt(ms + EPS)) * (1.0 + scale) + shift


def _store_modulated_rms(h_scr, x_ref, i, tm, s_lat, modb_ref, modc_ref, k_shift, k_scale):
    @pl.when((i + 1) * tm > s_lat)
    def _():
        shift, scale = _mod_rows(i, tm, s_lat, modb_ref, modc_ref, (k_shift, k_scale))
        h_scr[...] = _modulated_rms(x_ref[...], shift, scale).astype(BF16)

    @pl.when((i + 1) * tm <= s_lat)
    def _():
        h_scr[...] = _modulated_rms(x_ref[...], modb_ref[k_shift:k_shift + 1, :],
                                    modb_ref[k_scale:k_scale + 1, :]).astype(BF16)


def _inproj_kernel(x_ref, modb_ref, modc_ref, w_ref, g_ref, cos_ref, sin_ref, seg_ref,
                   oqkv_ref, omid_ref, ogate_ref, h_scr, *, tm, s_lat):
    i = pl.program_id(1)
    _store_modulated_rms(h_scr, x_ref, i, tm, s_lat, modb_ref, modc_ref, 0, 1)

    def proj(c0, width):
        return jnp.dot(h_scr[...], w_ref[:, c0:c0 + width], preferred_element_type=F32)

    acc = proj(0, QKV_WIDTH)
    cos = cos_ref[...]
    sin = sin_ref[...]
    lane = lax.broadcasted_iota(jnp.int32, (tm, LANES), 1)
    first = (lane % AXIS_DIM) < (AXIS_DIM // 2)
    for s in range(QK_WIDTH // LANES):
        cols = slice(s * LANES, (s + 1) * LANES)
        qk = acc[:, cols]
        ms = jnp.dot((qk * qk).astype(BF16), seg_ref[...], preferred_element_type=F32)
        ys = qk * lax.rsqrt(ms + EPS) * g_ref[:, cols]
        partner = jnp.where(first, pltpu.roll(ys, LANES - AXIS_DIM // 2, 1), pltpu.roll(ys, AXIS_DIM // 2, 1))
        oqkv_ref[:, s * LANES:(s + 1) * LANES] = (ys * cos + partner * sin).astype(BF16)
    oqkv_ref[:, QK_WIDTH:] = acc[:, QK_WIDTH:].astype(BF16)

    acc = proj(OFF_CONF, 2 * CONF_WIDTH)
    omid_ref[:, M_GLU:M_GLU + CONF_WIDTH] = (acc[:, :CONF_WIDTH] * _sigmoid(acc[:, CONF_WIDTH:])).astype(BF16)
    acc = proj(OFF_SC, 3 * SC_WIDTH)
    omid_ref[:, M_BG:M_BG + SC_WIDTH] = acc[:, :SC_WIDTH].astype(BF16)
    omid_ref[:, M_CX:M_CX + SC_WIDTH] = (acc[:, SC_WIDTH:2 * SC_WIDTH] * acc[:, 2 * SC_WIDTH:]).astype(BF16)
    for m in range(N_BRANCHES):
        acc = proj(OFF_GATE + m * D_MODEL, D_MODEL)
        ogate_ref[:, m * D_MODEL:(m + 1) * D_MODEL] = _sigmoid(acc).astype(BF16)


def _in_proj(xs, mod_l, w, gvec, cos_t, sin_t, seg, *, s_lat, tm):
    b, t, d = xs.shape
    nb = mod_l.shape[0] - 1
    kern = functools.partial(_inproj_kernel, tm=tm, s_lat=s_lat)
    tok = lambda width: pl.BlockSpec((None, tm, width), lambda bi, i: (bi, i, 0))
    return pl.pallas_call(
        kern,
        grid=(b, t // tm),
        in_specs=[
            tok(d),
            pl.BlockSpec((None, N_MOD, d), lambda bi, i: (bi, 0, 0)),
            pl.BlockSpec((None, N_MOD, d), lambda bi, i: (nb, 0, 0)),
            _resident((d, IN_WIDTH)),
            _resident((1, QK_WIDTH)),
            pl.BlockSpec((tm, LANES), lambda bi, i: (i, 0)),
            pl.BlockSpec((tm, LANES), lambda bi, i: (i, 0)),
            _resident((LANES, LANES)),
        ],
        out_specs=[tok(QKV_WIDTH), tok(MID_WIDTH), tok(GATE_WIDTH)],
        out_shape=[
            jax.ShapeDtypeStruct((b, t, QKV_WIDTH), BF16),
            jax.ShapeDtypeStruct((b, t, MID_WIDTH), BF16),
            jax.ShapeDtypeStruct((b, t, GATE_WIDTH), BF16),
        ],
        scratch_shapes=[pltpu.VMEM((tm, d), BF16)],
        compiler_params=_cparams(("arbitrary", "arbitrary")),
        name="in_proj",
    )(xs, mod_l, mod_l, w, gvec, cos_t, sin_t, seg)


def _dependent_zero(x, shape, dtype):
    bits = pltpu.bitcast(x, jnp.int32)
    z = lax.shift_right_logical(lax.shift_right_logical(bits, 16), 16).astype(F32)
    while z.shape[1] > shape[1]:
        half = z.shape[1] // 2
        z = z[:, :half] + z[:, half:]
    while z.shape[0] < shape[0]:
        z = jnp.concatenate([z, z], axis=0)
    return z.astype(dtype)


def _conformer_conv_parts(u, j, n_seq_sub, m_ref, mp_ref, mn_ref, cw_ref, cb_ref, lg_ref, lb_ref, hn_ref, hp, hc, sh):
    sub, n_sub = ATTN_ROWS, m_ref.shape[0] // ATTN_ROWS
    r_lo, r_hi = u * sub, (u + 1) * sub
    rows = 128
    first_off = HALO - CONF_KERNEL // 2
    reach = (first_off + CONF_KERNEL - 1) // SUBLANES * SUBLANES

    def fill():
        prev = mp_ref[...] if u == 0 else m_ref[r_lo - HALO:r_lo, :]
        nxt = mn_ref[...] if u == n_sub - 1 else m_ref[r_hi:r_hi + HALO, :]
        hp[0:HALO, :] = jnp.where(j != 0, prev.astype(F32), 0.0)
        hp[HALO:HALO + sub, :] = m_ref[r_lo:r_hi, :].astype(F32)
        hp[HALO + sub:, :] = jnp.where(j != n_seq_sub - 1, nxt.astype(F32), 0.0)

    def shifts(c0):
        for r in range(1, SUBLANES):
            sh[c0 // LANES, r - 1] = hp[r:r + sub + reach, c0:c0 + LANES]

    def taps(c0, r0):
        acc = jnp.zeros((rows, LANES), F32)
        for k in range(CONF_KERNEL):
            r = (first_off + k) % SUBLANES
            a8 = r0 + (first_off + k) // SUBLANES * SUBLANES
            src = hp[a8:a8 + rows, c0:c0 + LANES] if r == 0 else sh[c0 // LANES, r - 1, a8:a8 + rows, :]
            acc = acc + cw_ref[k:k + 1, c0:c0 + LANES] * src
        hc[r0:r0 + rows, c0:c0 + LANES] = acc
        return acc

    def norm():
        hcv = hc[...] + cb_ref[...]
        mu = jnp.mean(hcv, axis=-1, keepdims=True)
        var = jnp.mean(jnp.square(hcv - mu), axis=-1, keepdims=True)
        hn = (hcv - mu) * lax.rsqrt(var + EPS) * lg_ref[...] + lb_ref[...]
        hn = hn * _sigmoid(hn)
        hn_ref[r_lo:r_hi, :] = hn.astype(BF16)
        return hn

    parts = [fill]
    for c0 in range(0, CONF_WIDTH, LANES):
        parts.append(functools.partial(shifts, c0))
        parts += [functools.partial(taps, c0, r0) for r0 in range(0, sub, rows)]
    return parts + [norm]


def _attn_kernel(q_ref, k_ref, v_ref, m_ref, mp_ref, mn_ref, cw_ref, cb_ref, lg_ref, lb_ref, *rest, tq, nk, n_seq_sub):
    o_ref, hn_ref, kd_scr, ve_scr, hp_scr, hc_scr, sh_scr = rest[-7:]
    i = pl.program_id(1)
    n_sub = tq // ATTN_ROWS

    @pl.when(i == 0)
    def _():
        lo = lax.broadcasted_iota(jnp.int32, (nk, LANES), 1) < HEAD_DIM
        for src, dst in ((k_ref, kd_scr), (v_ref, ve_scr)):
            x = src[...].astype(F32)
            xs = pltpu.roll(x, HEAD_DIM, 1)
            dst[0, :, 0:LANES] = jnp.where(lo, x, xs).astype(BF16)
            dst[1, :, 0:LANES] = jnp.where(lo, xs, x).astype(BF16)
        for g in range(N_KV_HEADS):
            ve_scr[g, :, LANES:] = jnp.ones((nk, LANES), BF16)

    lo = lax.broadcasted_iota(jnp.int32, (ATTN_ROWS, LANES), 1) < HEAD_DIM
    pending = []
    for u in range(n_sub):
        r0 = u * ATTN_ROWS
        pending += _conformer_conv_parts(u, i * n_sub + u, n_seq_sub, m_ref, mp_ref, mn_ref, cw_ref, cb_ref, lg_ref,
                                         lb_ref, hn_ref, hp_scr.at[u], hc_scr.at[u], sh_scr.at[u])
        for p in range(ATTN_WIDTH // LANES):
            g = (p * LANES // HEAD_DIM) // Q_GROUP
            qt = q_ref[r0:r0 + ATTN_ROWS, p * LANES:(p + 1) * LANES]
            outs = []
            for mask in (lo, jnp.logical_not(lo)):
                qm = jnp.where(mask, qt, jnp.zeros_like(qt))
                s = lax.dot_general(qm, kd_scr[g], (((1,), (1,)), ((), ())), preferred_element_type=F32)
                m = jnp.max(s, axis=-1, keepdims=True)
                e = jnp.exp2(s - m).astype(BF16)
                pv = jnp.dot(e, ve_scr[g], preferred_element_type=F32)
                o = pv[:, :LANES] / pv[:, LANES:]
                heads_left = (n_sub - u) * N_Q_HEADS - 2 * p - len(outs)
                n_now = -(-len(pending) // heads_left)
                for part in pending[:n_now]:
                    done = part()
                    if done is not None:
                        o = o + _dependent_zero(done, o.shape, o.dtype)
                pending = pending[n_now:]
                outs.append(o)
            o_ref[r0:r0 + ATTN_ROWS, p * LANES:(p + 1) * LANES] = jnp.where(lo, outs[0], outs[1]).astype(BF16)
    assert not pending


def _attention(qkv, mid, prev_outs, cw, cb, lg, lb, *, q_row0, n_q, tq, k_row0, n_k):
    b, t, _ = qkv.shape
    qb0, kb = q_row0 // tq, k_row0 // n_k
    hb0, hb = q_row0 // HALO, tq // HALO
    n_sub = tq // ATTN_ROWS
    row_blk = lambda bi, i: (bi, qb0 + i, 0)
    in_specs = [
        pl.BlockSpec((None, tq, ATTN_WIDTH), row_blk),
        pl.BlockSpec((None, n_k, KV_WIDTH), lambda bi, i: (bi, kb, ATTN_WIDTH // KV_WIDTH)),
        pl.BlockSpec((None, n_k, KV_WIDTH), lambda bi, i: (bi, kb, QK_WIDTH // KV_WIDTH)),
        pl.BlockSpec((None, tq, CONF_WIDTH), row_blk),
        pl.BlockSpec((None, HALO, CONF_WIDTH), lambda bi, i: (bi, jnp.maximum(hb0 + i * hb - 1, 0), 0)),
        pl.BlockSpec((None, HALO, CONF_WIDTH), lambda bi, i: (bi, jnp.minimum(hb0 + (i + 1) * hb, t // HALO - 1), 0)),
        _resident((CONF_KERNEL, CONF_WIDTH)),
        _resident((1, CONF_WIDTH)),
        _resident((1, CONF_WIDTH)),
        _resident((1, CONF_WIDTH)),
    ]
    args = [qkv, qkv, qkv, mid, mid, mid, cw, cb, lg, lb]
    aliases = {}
    if prev_outs is not None:
        aliases = {len(args): 0, len(args) + 1: 1}
        in_specs += [pl.BlockSpec(memory_space=pl.ANY)] * 2
        args += list(prev_outs)
    kern = functools.partial(_attn_kernel, tq=tq, nk=n_k, n_seq_sub=n_q // ATTN_ROWS)
    out = jax.ShapeDtypeStruct((b, t, ATTN_WIDTH), BF16)
    return pl.pallas_call(
        kern,
        grid=(b, n_q // tq),
        in_specs=in_specs,
        out_specs=[pl.BlockSpec((None, tq, ATTN_WIDTH), row_blk)] * 2,
        out_shape=[out, out],
        input_output_aliases=aliases,
        scratch_shapes=[pltpu.VMEM((N_KV_HEADS, n_k, LANES), BF16),
                        pltpu.VMEM((N_KV_HEADS, n_k, 2 * LANES), BF16),
                        pltpu.VMEM((n_sub, ATTN_ROWS + 2 * HALO, CONF_WIDTH), F32),
                        pltpu.VMEM((n_sub, ATTN_ROWS, CONF_WIDTH), F32),
                        pltpu.VMEM((n_sub, CONF_WIDTH // LANES, SUBLANES - 1, ATTN_ROWS + 2 * HALO - SUBLANES, LANES),
                                   F32)],
        compiler_params=_cparams(("arbitrary", "arbitrary")),
        name="attention",
    )(*args)


def _merge_kernel(x_ref, m_ref, mp_ref, mn_ref, gt_ref, a_ref, hn_ref, modb_ref, modc_ref,
                  wao_ref, wco_ref, wso_ref, wmix_ref, sw_ref, o_ref, up_scr, *, tm, s_lat, t):
    i = pl.program_id(1)
    sub = ATTN_ROWS
    n_sub = tm // sub
    n_lat = s_lat // sub
    n_all = t // sub

    for u in range(n_sub):
        j = i * n_sub + u
        r_lo, r_hi = u * sub, (u + 1) * sub
        main = slice(r_lo, r_hi)
        prev_ok = jnp.logical_and(j != 0, j != n_lat)
        next_ok = jnp.logical_and(j != n_lat - 1, j != n_all - 1)
        up = up_scr.at[u]

        prev = mp_ref[...] if u == 0 else m_ref[r_lo - HALO:r_lo, M_CX:M_CX + SC_WIDTH]
        nxt = mn_ref[...] if u == n_sub - 1 else m_ref[r_hi:r_hi + HALO, M_CX:M_CX + SC_WIDTH]
        up[0:HALO, :] = jnp.where(prev_ok, prev.astype(F32), 0.0)
        up[HALO:HALO + sub, :] = m_ref[main, M_CX:M_CX + SC_WIDTH].astype(F32)
        up[HALO + sub:, :] = jnp.where(next_ok, nxt.astype(F32), 0.0)

        y_attn = jnp.dot(a_ref[main, :], wao_ref[...], preferred_element_type=F32)
        y_conf = jnp.dot(hn_ref[main, :], wco_ref[...], preferred_element_type=F32)
        sacc = jnp.zeros((sub, SC_WIDTH), F32)
        for k in range(SC_KERNEL):
            off = HALO - SC_KERNEL // 2 + k
            sacc = sacc + sw_ref[k:k + 1, :] * up[off:off + sub, :]
        bg = m_ref[main, M_BG:M_BG + SC_WIDTH].astype(F32)
        y_sc = jnp.dot((bg * sacc).astype(BF16), wso_ref[...], preferred_element_type=F32)

        def gate(m):
            return gt_ref[main, m * D_MODEL:(m + 1) * D_MODEL].astype(F32)

        merged = gate(0) * y_attn + gate(1) * y_conf + gate(2) * y_sc
        mixed = jnp.dot(merged.astype(BF16), wmix_ref[...], preferred_element_type=F32)
        g_res = jnp.where(j >= n_lat, modc_ref[2:3, :], modb_ref[2:3, :])
        o_ref[main, :] = x_ref[main, :] + g_res * mixed


def _merge(xs, mid, gates, attn, hn, mod_l, wao, wco, wso, wmix, sw, *, s_lat, tm, out_rows):
    b, t, d = xs.shape
    nb = mod_l.shape[0] - 1
    hb = tm // HALO
    n_hblk = t // HALO
    cx_blk = M_CX // SC_WIDTH
    kern = functools.partial(_merge_kernel, tm=tm, s_lat=s_lat, t=t)
    sub, n_sub = ATTN_ROWS, tm // ATTN_ROWS
    tok = lambda width: pl.BlockSpec((None, tm, width), lambda bi, i: (bi, i, 0))
    return pl.pallas_call(
        kern,
        grid=(b, out_rows // tm),
        in_specs=[
            tok(d),
            tok(MID_WIDTH),
            pl.BlockSpec((None, HALO, SC_WIDTH), lambda bi, i: (bi, jnp.maximum(i * hb - 1, 0), cx_blk)),
            pl.BlockSpec((None, HALO, SC_WIDTH), lambda bi, i: (bi, jnp.minimum((i + 1) * hb, n_hblk - 1), cx_blk)),
            tok(GATE_WIDTH),
            tok(ATTN_WIDTH),
            tok(CONF_WIDTH),
            pl.BlockSpec((None, N_MOD, d), lambda bi, i: (bi, 0, 0)),
            pl.BlockSpec((None, N_MOD, d), lambda bi, i: (nb, 0, 0)),
            _resident((ATTN_WIDTH, d)),
            _resident((CONF_WIDTH, d)),
            _resident((SC_WIDTH, d)),
            _resident((d, d)),
            _resident((SC_KERNEL, SC_WIDTH)),
        ],
        out_specs=tok(d),
        out_shape=jax.ShapeDtypeStruct((b, out_rows, d), F32),
        scratch_shapes=[pltpu.VMEM((n_sub, sub + 2 * HALO, SC_WIDTH), F32)],
        compiler_params=_cparams(("arbitrary", "arbitrary")),
        name="merge",
    )(xs, mid, mid, mid, gates, attn, hn, mod_l, mod_l, wao, wco, wso, wmix, sw)


def _ffn_kernel(x_ref, modb_ref, modc_ref, wa_ref, wb_ref, wo_ref, o_ref, h_scr, u_scr, *, tm, s_lat):
    i = pl.program_id(1)
    _store_modulated_rms(h_scr, x_ref, i, tm, s_lat, modb_ref, modc_ref, 3, 4)
    for c0 in range(0, FFN_HIDDEN, FFN_TH):
        cols = slice(c0, min(c0 + FFN_TH, FFN_HIDDEN))
        a = jnp.dot(h_scr[...], wa_ref[:, cols], preferred_element_type=F32)
        bb = jnp.dot(h_scr[...], wb_ref[:, cols], preferred_element_type=F32)
        u_scr[:, cols] = (a * _sigmoid(a) * bb).astype(BF16)
    y = jnp.dot(u_scr[...], wo_ref[...], preferred_element_type=F32)
    (g_res,) = _mod_rows(i, tm, s_lat, modb_ref, modc_ref, (5,))
    o_ref[...] = x_ref[...] + g_res * y


def _ffn(xs, mod_l, w_in, w_out, *, s_lat, tm):
    b, t, d = xs.shape
    nb = mod_l.shape[0] - 1
    kern = functools.partial(_ffn_kernel, tm=tm, s_lat=s_lat)
    tok = pl.BlockSpec((None, tm, d), lambda bi, i: (bi, i, 0))
    return pl.pallas_call(
        kern,
        grid=(b, t // tm),
        in_specs=[
            tok,
            pl.BlockSpec((None, N_MOD, d), lambda bi, i: (bi, 0, 0)),
            pl.BlockSpec((None, N_MOD, d), lambda bi, i: (nb, 0, 0)),
            pl.BlockSpec((d, FFN_HIDDEN), lambda bi, i: (0, 0), pipeline_mode=pl.Buffered(1)),
            pl.BlockSpec((d, FFN_HIDDEN), lambda bi, i: (0, 1), pipeline_mode=pl.Buffered(1)),
            _resident((FFN_HIDDEN, d)),
        ],
        out_specs=tok,
        out_shape=jax.ShapeDtypeStruct((b, t, d), F32),
        scratch_shapes=[pltpu.VMEM((tm, d), BF16), pltpu.VMEM((tm, FFN_HIDDEN), BF16)],
        compiler_params=_cparams(("arbitrary", "arbitrary")),
        name="ffn",
    )(xs, mod_l, mod_l, w_in, w_in, w_out)


def _rope_tables(s, lc):
    pos = jnp.arange(s)
    r_ids = (pos // GRID_W).astype(F32)
    c_ids = (pos % GRID_W).astype(F32)
    freqs = ROPE_THETA ** (-jnp.arange(0, AXIS_DIM, 2, dtype=F32) / AXIS_DIM)
    ang_r = r_ids[:, None] * freqs
    ang_c = c_ids[:, None] * freqs
    cos_h = jnp.concatenate([jnp.cos(ang_r), jnp.cos(ang_r), jnp.cos(ang_c), jnp.cos(ang_c)], axis=1)
    sin_h = jnp.concatenate([-jnp.sin(ang_r), jnp.sin(ang_r), -jnp.sin(ang_c), jnp.sin(ang_c)], axis=1)
    reps = LANES // HEAD_DIM
    cos_t = jnp.concatenate([jnp.tile(cos_h, (1, reps)), jnp.ones((lc, LANES), F32)], axis=0)
    sin_t = jnp.concatenate([jnp.tile(sin_h, (1, reps)), jnp.zeros((lc, LANES), F32)], axis=0)
    return cos_t, sin_t


def kernel(x, c, ctx, c_ctx, w_ada, b_ada, w_in, q_norm, k_norm, w_attn_o, conf_dw_w, conf_dw_b,
           conf_ln_g, conf_ln_b, w_conf_out, sc_dw_w, w_sc_out, w_mix_out, w_ffn_in, w_ffn_out):
    b, s, d = x.shape
    lc = ctx.shape[1]
    n_layers = w_ada.shape[0]
    assert d == D_MODEL and lc == ATTN_ROWS and s % 1024 == 0 and (s + lc) % 768 == 0

    xs = jnp.concatenate([x, ctx], axis=1)

    mp = -(-(b + 1) // SUBLANES) * SUBLANES
    cc = jnp.concatenate([c, jnp.zeros((mp - b - 1, d), F32), c_ctx[None, :]], axis=0)
    mod = _modulation(cc, w_ada, b_ada).reshape(n_layers, mp, N_MOD, d)

    cos_t, sin_t = _rope_tables(s, lc)
    head_of_lane = jnp.arange(LANES) // HEAD_DIM
    seg = (head_of_lane[:, None] == head_of_lane[None, :]).astype(BF16) * (1.0 / HEAD_DIM)
    w_in, w_attn_o, w_conf_out, w_sc_out, w_mix_out, w_ffn_in, w_ffn_out = (
        w.astype(BF16) for w in (w_in, w_attn_o, w_conf_out, w_sc_out, w_mix_out, w_ffn_in, w_ffn_out))

    for i in range(n_layers):
        gvec = jnp.concatenate([jnp.tile(q_norm[i] * (ATTN_SCALE * LOG2E), N_Q_HEADS),
                                jnp.tile(k_norm[i], N_KV_HEADS)])[None, :]
        last = i == n_layers - 1
        qkv, mid, gates = _in_proj(xs, mod[i], w_in[i], gvec, cos_t, sin_t, seg, s_lat=s, tm=768)
        conv_w = (conf_dw_w[i], conf_dw_b[i][None, :], conf_ln_g[i][None, :], conf_ln_b[i][None, :])
        outs = _attention(qkv, mid, None, *conv_w, q_row0=0, n_q=s, tq=1024, k_row0=0, n_k=s + lc)
        if not last:
            outs = _attention(qkv, mid, outs, *conv_w, q_row0=s, n_q=lc, tq=lc, k_row0=s, n_k=lc)
        attn, hn = outs
        xs = _merge(xs, mid, gates, attn, hn, mod[i], w_attn_o[i], w_conf_out[i], w_sc_out[i], w_mix_out[i],
                    sc_dw_w[i], s_lat=s, tm=512 if last else 768, out_rows=s if last else s + lc)
        xs = _ffn(xs, mod[i], w_ffn_in[i], w_ffn_out[i], s_lat=s, tm=512 if last else 768)
    return xs
```

```python
import functools

import jax
import jax.numpy as jnp
from jax import lax
from jax.experimental import pallas as pl
from jax.experimental.pallas import tpu as pltpu

F32 = jnp.float32
BF16 = jnp.bfloat16

D_MODEL = 1024
HEAD_DIM = 64
N_Q_HEADS = 8
N_KV_HEADS = 2
Q_GROUP = N_Q_HEADS // N_KV_HEADS
ATTN_WIDTH = N_Q_HEADS * HEAD_DIM
KV_WIDTH = N_KV_HEADS * HEAD_DIM
AXIS_DIM = HEAD_DIM // 2
ROPE_THETA = 10000.0
GRID_W = 64
ATTN_SCALE = HEAD_DIM ** -0.5
CONF_WIDTH = D_MODEL // 2
CONF_KERNEL = 31
SC_WIDTH = D_MODEL // 2
SC_KERNEL = 3
N_BRANCHES = 3
N_MOD = 6
FFN_HIDDEN = 2816
EPS = 1e-6

OFF_K = ATTN_WIDTH
OFF_V = OFF_K + KV_WIDTH
OFF_CONF = OFF_V + KV_WIDTH
OFF_SC = OFF_CONF + 2 * CONF_WIDTH
OFF_GATE = OFF_SC + 3 * SC_WIDTH
IN_WIDTH = OFF_GATE + N_BRANCHES * D_MODEL

LANES = 128
SUBLANES = 8
MXU_N = 256
QK_WIDTH = ATTN_WIDTH + KV_WIDTH
QKV_WIDTH = QK_WIDTH + KV_WIDTH
MID_WIDTH = CONF_WIDTH + 2 * SC_WIDTH
M_GLU = 0
M_CX = CONF_WIDTH
M_BG = CONF_WIDTH + SC_WIDTH
GATE_WIDTH = N_BRANCHES * D_MODEL
HALO = 16
FFN_TH = 2 * MXU_N
ATTN_ROWS = 256
LOG2E = 1.4426950408889634
VMEM_LIMIT = 56 * 1024 * 1024


def _cparams(sem):
    return pltpu.CompilerParams(dimension_semantics=sem, vmem_limit_bytes=VMEM_LIMIT)


def _resident(shape):
    return pl.BlockSpec(shape, lambda *_: (0,) * len(shape), pipeline_mode=pl.Buffered(1))


def _sigmoid(x):
    return 1.0 / (1.0 + jnp.exp(-x))


def _mod_kernel(c_ref, w_ref, b_ref, o_ref):
    a = c_ref[...]
    a = (a * _sigmoid(a)).astype(BF16)
    o_ref[...] = jnp.dot(a, w_ref[...].astype(BF16), preferred_element_type=F32) + b_ref[...]


def _modulation(cc, w_ada, b_ada):
    n_layers, d, n = w_ada.shape
    mp = cc.shape[0]
    tn = 1536
    return pl.pallas_call(
        _mod_kernel,
        grid=(n_layers, n // tn),
        in_specs=[
            pl.BlockSpec((mp, d), lambda l, j: (0, 0)),
            pl.BlockSpec((None, d, tn), lambda l, j: (l, 0, j)),
            pl.BlockSpec((None, 1, tn), lambda l, j: (l, 0, j)),
        ],
        out_specs=pl.BlockSpec((None, mp, tn), lambda l, j: (l, 0, j)),
        out_shape=jax.ShapeDtypeStruct((n_layers, mp, n), F32),
        compiler_params=_cparams(("arbitrary", "arbitrary")),
        name="adaln_mod",
    )(cc, w_ada, b_ada.reshape(n_layers, 1, n))


def _mod_rows(i, tm, s_lat, modb_ref, modc_ref, idx):
    row = i * tm + lax.broadcasted_iota(jnp.int32, (tm, 1), 0)
    is_ctx = row >= s_lat
    return [jnp.where(is_ctx, modc_ref[k:k + 1, :], modb_ref[k:k + 1, :]) for k in idx]


def _modulated_rms(x, shift, scale):
    ms = jnp.mean(x * x, axis=-1, keepdims=True)
    return (x * lax.rsqrt(ms + EPS)) * (1.0 + scale) + shift


def _store_modulated_rms(h_scr, x_ref, i, tm, s_lat, modb_ref, modc_ref, k_shift, k_scale):
    @pl.when((i + 1) * tm > s_lat)
    def _():
        shift, scale = _mod_rows(i, tm, s_lat, modb_ref, modc_ref, (k_shift, k_scale))
        h_scr[...] = _modulated_rms(x_ref[...], shift, scale).astype(BF16)

    @pl.when((i + 1) * tm <= s_lat)
    def _():
        h_scr[...] = _modulated_rms(x_ref[...], modb_ref[k_shift:k_shift + 1, :],
                                    modb_ref[k_scale:k_scale + 1, :]).astype(BF16)


def _inproj_kernel(x_ref, modb_ref, modc_ref, w_ref, g_ref, cos_ref, sin_ref, seg_ref,
                   oqkv_ref, omid_ref, ogate_ref, h_scr, *, tm, s_lat):
    i = pl.program_id(1)
    sub = ATTN_ROWS
    n_sub = tm // sub
    lane = lax.broadcasted_iota(jnp.int32, (sub, LANES), 1)
    first = (lane % AXIS_DIM) < (AXIS_DIM // 2)

    for u in range(n_sub):
        rows = slice(u * sub, (u + 1) * sub)
        is_ctx = (i * n_sub + u) * sub >= s_lat
        shift = jnp.where(is_ctx, modc_ref[0:1, :], modb_ref[0:1, :])
        scale = jnp.where(is_ctx, modc_ref[1:2, :], modb_ref[1:2, :])
        h_scr[rows, :] = _modulated_rms(x_ref[rows, :], shift, scale).astype(BF16)

        def proj(c0, width):
            return jnp.dot(h_scr[rows, :], w_ref[:, c0:c0 + width], preferred_element_type=F32)

        acc = proj(0, QKV_WIDTH)
        cos = cos_ref[rows, :]
        sin = sin_ref[rows, :]
        for s in range(QK_WIDTH // LANES):
            cols = slice(s * LANES, (s + 1) * LANES)
            qk = acc[:, cols]
            ms = jnp.dot((qk * qk).astype(BF16), seg_ref[...], preferred_element_type=F32)
            ys = qk * lax.rsqrt(ms + EPS) * g_ref[:, cols]
            partner = jnp.where(first, pltpu.roll(ys, LANES - AXIS_DIM // 2, 1), pltpu.roll(ys, AXIS_DIM // 2, 1))
            oqkv_ref[rows, cols] = (ys * cos + partner * sin).astype(BF16)
        oqkv_ref[rows, QK_WIDTH:] = acc[:, QK_WIDTH:].astype(BF16)

        acc = proj(OFF_CONF, 2 * CONF_WIDTH)
        omid_ref[rows, M_GLU:M_GLU + CONF_WIDTH] = (acc[:, :CONF_WIDTH] * _sigmoid(acc[:, CONF_WIDTH:])).astype(BF16)
        acc = proj(OFF_SC, 3 * SC_WIDTH)
        omid_ref[rows, M_BG:M_BG + SC_WIDTH] = acc[:, :SC_WIDTH].astype(BF16)
        omid_ref[rows, M_CX:M_CX + SC_WIDTH] = (acc[:, SC_WIDTH:2 * SC_WIDTH] * acc[:, 2 * SC_WIDTH:]).astype(BF16)
        for m in range(N_BRANCHES):
            acc = proj(OFF_GATE + m * D_MODEL, D_MODEL)
            ogate_ref[rows, m * D_MODEL:(m + 1) * D_MODEL] = _sigmoid(acc).astype(BF16)


def _in_proj(xs, mod_l, w, gvec, cos_t, sin_t, seg, *, s_lat, tm):
    b, t, d = xs.shape
    nb = mod_l.shape[0] - 1
    kern = functools.partial(_inproj_kernel, tm=tm, s_lat=s_lat)
    tok = lambda width: pl.BlockSpec((None, tm, width), lambda bi, i: (bi, i, 0))
    return pl.pallas_call(
        kern,
        grid=(b, t // tm),
        in_specs=[
            tok(d),
            pl.BlockSpec((None, N_MOD, d), lambda bi, i: (bi, 0, 0)),
            pl.BlockSpec((None, N_MOD, d), lambda bi, i: (nb, 0, 0)),
            _resident((d, IN_WIDTH)),
            _resident((1, QK_WIDTH)),
            pl.BlockSpec((tm, LANES), lambda bi, i: (i, 0)),
            pl.BlockSpec((tm, LANES), lambda bi, i: (i, 0)),
            _resident((LANES, LANES)),
        ],
        out_specs=[tok(QKV_WIDTH), tok(MID_WIDTH), tok(GATE_WIDTH)],
        out_shape=[
            jax.ShapeDtypeStruct((b, t, QKV_WIDTH), BF16),
            jax.ShapeDtypeStruct((b, t, MID_WIDTH), BF16),
            jax.ShapeDtypeStruct((b, t, GATE_WIDTH), BF16),
        ],
        scratch_shapes=[pltpu.VMEM((tm, d), BF16)],
        compiler_params=_cparams(("arbitrary", "arbitrary")),
        name="in_proj",
    )(xs, mod_l, mod_l, w, gvec, cos_t, sin_t, seg)


def _dependent_zero(x, shape, dtype):
    bits = pltpu.bitcast(x, jnp.int32)
    z = lax.shift_right_logical(lax.shift_right_logical(bits, 16), 16).astype(F32)
    while z.shape[1] > shape[1]:
        half = z.shape[1] // 2
        z = z[:, :half] + z[:, half:]
    while z.shape[0] < shape[0]:
        z = jnp.concatenate([z, z], axis=0)
    return z.astype(dtype)


def _conformer_conv_parts(u, j, n_seq_sub, m_ref, mp_ref, mn_ref, cw_ref, cb_ref, lg_ref, lb_ref, hn_ref, hp, hc, sh):
    sub, n_sub = ATTN_ROWS, m_ref.shape[0] // ATTN_ROWS
    r_lo, r_hi = u * sub, (u + 1) * sub
    rows = 128
    first_off = HALO - CONF_KERNEL // 2
    reach = (first_off + CONF_KERNEL - 1) // SUBLANES * SUBLANES

    def fill():
        prev = mp_ref[...] if u == 0 else m_ref[r_lo - HALO:r_lo, :]
        nxt = mn_ref[...] if u == n_sub - 1 else m_ref[r_hi:r_hi + HALO, :]
        hp[0:HALO, :] = jnp.where(j != 0, prev.astype(F32), 0.0)
        hp[HALO:HALO + sub, :] = m_ref[r_lo:r_hi, :].astype(F32)
        hp[HALO + sub:, :] = jnp.where(j != n_seq_sub - 1, nxt.astype(F32), 0.0)

    def shifts(c0):
        for r in range(1, SUBLANES):
            sh[c0 // LANES, r - 1] = hp[r:r + sub + reach, c0:c0 + LANES]

    def taps(c0, r0):
        acc = jnp.zeros((rows, LANES), F32)
        for k in range(CONF_KERNEL):
            r = (first_off + k) % SUBLANES
            a8 = r0 + (first_off + k) // SUBLANES * SUBLANES
            src = hp[a8:a8 + rows, c0:c0 + LANES] if r == 0 else sh[c0 // LANES, r - 1, a8:a8 + rows, :]
            acc = acc + cw_ref[k:k + 1, c0:c0 + LANES] * src
        hc[r0:r0 + rows, c0:c0 + LANES] = acc
        return acc

    def norm():
        hcv = hc[...] + cb_ref[...]
        mu = jnp.mean(hcv, axis=-1, keepdims=True)
        var = jnp.mean(jnp.square(hcv - mu), axis=-1, keepdims=True)
        hn = (hcv - mu) * lax.rsqrt(var + EPS) * lg_ref[...] + lb_ref[...]
        hn = hn * _sigmoid(hn)
        hn_ref[r_lo:r_hi, :] = hn.astype(BF16)
        return hn

    parts = [fill]
    for c0 in range(0, CONF_WIDTH, LANES):
        parts.append(functools.partial(shifts, c0))
        parts += [functools.partial(taps, c0, r0) for r0 in range(0, sub, rows)]
    return parts + [norm]


def _attn_kernel(q_ref, k_ref, v_ref, m_ref, mp_ref, mn_ref, cw_ref, cb_ref, lg_ref, lb_ref, *rest, tq, nk, n_seq_sub):
    o_ref, hn_ref, kd_scr, ve_scr, hp_scr, hc_scr, sh_scr = rest[-7:]
    i = pl.program_id(1)
    n_sub = tq // ATTN_ROWS

    @pl.when(i == 0)
    def _():
        lo = lax.broadcasted_iota(jnp.int32, (nk, LANES), 1) < HEAD_DIM
        for src, dst in ((k_ref, kd_scr), (v_ref, ve_scr)):
            x = src[...].astype(F32)
            xs = pltpu.roll(x, HEAD_DIM, 1)
            dst[0, :, 0:LANES] = jnp.where(lo, x, xs).astype(BF16)
            dst[1, :, 0:LANES] = jnp.where(lo, xs, x).astype(BF16)
        for g in range(N_KV_HEADS):
            ve_scr[g, :, LANES:] = jnp.ones((nk, LANES), BF16)

    lo = lax.broadcasted_iota(jnp.int32, (ATTN_ROWS, LANES), 1) < HEAD_DIM
    pending = []
    for u in range(n_sub):
        r0 = u * ATTN_ROWS
        pending += _conformer_conv_parts(u, i * n_sub + u, n_seq_sub, m_ref, mp_ref, mn_ref, cw_ref, cb_ref, lg_ref,
                                         lb_ref, hn_ref, hp_scr.at[u], hc_scr.at[u], sh_scr.at[u])
        for p in range(ATTN_WIDTH // LANES):
            g = (p * LANES // HEAD_DIM) // Q_GROUP
            qt = q_ref[r0:r0 + ATTN_ROWS, p * LANES:(p + 1) * LANES]
            outs = []
            for mask in (lo, jnp.logical_not(lo)):
                qm = jnp.where(mask, qt, jnp.zeros_like(qt))
                s = lax.dot_general(qm, kd_scr[g], (((1,), (1,)), ((), ())), preferred_element_type=F32)
                m = jnp.max(s, axis=-1, keepdims=True)
                e = jnp.exp2(s - m).astype(BF16)
                pv = jnp.dot(e, ve_scr[g], preferred_element_type=F32)
                o = pv[:, :LANES] / pv[:, LANES:]
                heads_left = (n_sub - u) * N_Q_HEADS - 2 * p - len(outs)
                n_now = -(-len(pending) // heads_left)
                for part in pending[:n_now]:
                    done = part()
                    if done is not None:
                        o = o + _dependent_zero(done, o.shape, o.dtype)
                pending = pending[n_now:]
                outs.append(o)
            o_ref[r0:r0 + ATTN_ROWS, p * LANES:(p + 1) * LANES] = jnp.where(lo, outs[0], outs[1]).astype(BF16)
    assert not pending


def _attention(qkv, mid, prev_outs, cw, cb, lg, lb, *, q_row0, n_q, tq, k_row0, n_k):
    b, t, _ = qkv.shape
    qb0, kb = q_row0 // tq, k_row0 // n_k
    hb0, hb = q_row0 // HALO, tq // HALO
    n_sub = tq // ATTN_ROWS
    row_blk = lambda bi, i: (bi, qb0 + i, 0)
    in_specs = [
        pl.BlockSpec((None, tq, ATTN_WIDTH), row_blk),
        pl.BlockSpec((None, n_k, KV_WIDTH), lambda bi, i: (bi, kb, ATTN_WIDTH // KV_WIDTH)),
        pl.BlockSpec((None, n_k, KV_WIDTH), lambda bi, i: (bi, kb, QK_WIDTH // KV_WIDTH)),
        pl.BlockSpec((None, tq, CONF_WIDTH), row_blk),
        pl.BlockSpec((None, HALO, CONF_WIDTH), lambda bi, i: (bi, jnp.maximum(hb0 + i * hb - 1, 0), 0)),
        pl.BlockSpec((None, HALO, CONF_WIDTH), lambda bi, i: (bi, jnp.minimum(hb0 + (i + 1) * hb, t // HALO - 1), 0)),
        _resident((CONF_KERNEL, CONF_WIDTH)),
        _resident((1, CONF_WIDTH)),
        _resident((1, CONF_WIDTH)),
        _resident((1, CONF_WIDTH)),
    ]
    args = [qkv, qkv, qkv, mid, mid, mid, cw, cb, lg, lb]
    aliases = {}
    if prev_outs is not None:
        aliases = {len(args): 0, len(args) + 1: 1}
        in_specs += [pl.BlockSpec(memory_space=pl.ANY)] * 2
        args += list(prev_outs)
    kern = functools.partial(_attn_kernel, tq=tq, nk=n_k, n_seq_sub=n_q // ATTN_ROWS)
    out = jax.ShapeDtypeStruct((b, t, ATTN_WIDTH), BF16)
    return pl.pallas_call(
        kern,
        grid=(b, n_q // tq),
        in_specs=in_specs,
        out_specs=[pl.BlockSpec((None, tq, ATTN_WIDTH), row_blk)] * 2,
        out_shape=[out, out],
        input_output_aliases=aliases,
        scratch_shapes=[pltpu.VMEM((N_KV_HEADS, n_k, LANES), BF16),
                        pltpu.VMEM((N_KV_HEADS, n_k, 2 * LANES), BF16),
                        pltpu.VMEM((n_sub, ATTN_ROWS + 2 * HALO, CONF_WIDTH), F32),
                        pltpu.VMEM((n_sub, ATTN_ROWS, CONF_WIDTH), F32),
                        pltpu.VMEM((n_sub, CONF_WIDTH // LANES, SUBLANES - 1, ATTN_ROWS + 2 * HALO - SUBLANES, LANES),
                                   F32)],
        compiler_params=_cparams(("arbitrary", "arbitrary")),
        name="attention",
    )(*args)


def _merge_kernel(x_ref, m_ref, mp_ref, mn_ref, gt_ref, a_ref, hn_ref, modb_ref, modc_ref,
                  wao_ref, wco_ref, wso_ref, wmix_ref, sw_ref, o_ref, up_scr, *, tm, s_lat, t):
    i = pl.program_id(1)
    sub = ATTN_ROWS
    n_sub = tm // sub
    n_lat = s_lat // sub
    n_all = t // sub

    for u in range(n_sub):
        j = i * n_sub + u
        r_lo, r_hi = u * sub, (u + 1) * sub
        main = slice(r_lo, r_hi)
        prev_ok = jnp.logical_and(j != 0, j != n_lat)
        next_ok = jnp.logical_and(j != n_lat - 1, j != n_all - 1)
        up = up_scr.at[u]

        prev = mp_ref[...] if u == 0 else m_ref[r_lo - HALO:r_lo, M_CX:M_CX + SC_WIDTH]
        nxt = mn_ref[...] if u == n_sub - 1 else m_ref[r_hi:r_hi + HALO, M_CX:M_CX + SC_WIDTH]
        up[0:HALO, :] = jnp.where(prev_ok, prev.astype(F32), 0.0)
        up[HALO:HALO + sub, :] = m_ref[main, M_CX:M_CX + SC_WIDTH].astype(F32)
        up[HALO + sub:, :] = jnp.where(next_ok, nxt.astype(F32), 0.0)

        y_attn = jnp.dot(a_ref[main, :], wao_ref[...], preferred_element_type=F32)
        y_conf = jnp.dot(hn_ref[main, :], wco_ref[...], preferred_element_type=F32)
        sacc = jnp.zeros((sub, SC_WIDTH), F32)
        for k in range(SC_KERNEL):
            off = HALO - SC_KERNEL // 2 + k
            sacc = sacc + sw_ref[k:k + 1, :] * up[off:off + sub, :]
        bg = m_ref[main, M_BG:M_BG + SC_WIDTH].astype(F32)
        y_sc = jnp.dot((bg * sacc).astype(BF16), wso_ref[...], preferred_element_type=F32)

        def gate(m):
            return gt_ref[main, m * D_MODEL:(m + 1) * D_MODEL].astype(F32)

        merged = gate(0) * y_attn + gate(1) * y_conf + gate(2) * y_sc
        mixed = jnp.dot(merged.astype(BF16), wmix_ref[...], preferred_element_type=F32)
        g_res = jnp.where(j >= n_lat, modc_ref[2:3, :], modb_ref[2:3, :])
        o_ref[main, :] = x_ref[main, :] + g_res * mixed


def _merge(xs, mid, gates, attn, hn, mod_l, wao, wco, wso, wmix, sw, *, s_lat, tm, out_rows):
    b, t, d = xs.shape
    nb = mod_l.shape[0] - 1
    hb = tm // HALO
    n_hblk = t // HALO
    cx_blk = M_CX // SC_WIDTH
    kern = functools.partial(_merge_kernel, tm=tm, s_lat=s_lat, t=t)
    sub, n_sub = ATTN_ROWS, tm // ATTN_ROWS
    tok = lambda width: pl.BlockSpec((None, tm, width), lambda bi, i: (bi, i, 0))
    return pl.pallas_call(
        kern,
        grid=(b, out_rows // tm),
        in_specs=[
            tok(d),
            tok(MID_WIDTH),
            pl.BlockSpec((None, HALO, SC_WIDTH), lambda bi, i: (bi, jnp.maximum(i * hb - 1, 0), cx_blk)),
            pl.BlockSpec((None, HALO, SC_WIDTH), lambda bi, i: (bi, jnp.minimum((i + 1) * hb, n_hblk - 1), cx_blk)),
            tok(GATE_WIDTH),
            tok(ATTN_WIDTH),
            tok(CONF_WIDTH),
            pl.BlockSpec((None, N_MOD, d), lambda bi, i: (bi, 0, 0)),
            pl.BlockSpec((None, N_MOD, d), lambda bi, i: (nb, 0, 0)),
            _resident((ATTN_WIDTH, d)),
            _resident((CONF_WIDTH, d)),
            _resident((SC_WIDTH, d)),
            _resident((d, d)),
            _resident((SC_KERNEL, SC_WIDTH)),
        ],
        out_specs=tok(d),
        out_shape=jax.ShapeDtypeStruct((b, out_rows, d), F32),
        scratch_shapes=[pltpu.VMEM((n_sub, sub + 2 * HALO, SC_WIDTH), F32)],
        compiler_params=_cparams(("arbitrary", "arbitrary")),
        name="merge",
    )(xs, mid, mid, mid, gates, attn, hn, mod_l, mod_l, wao, wco, wso, wmix, sw)


def _ffn_kernel(x_ref, modb_ref, modc_ref, wa_ref, wb_ref, wo_ref, o_ref, h_scr, u_scr, *, tm, s_lat):
    i = pl.program_id(1)
    sub = ATTN_ROWS
    n_sub = tm // sub
    for u in range(n_sub):
        rows = slice(u * sub, (u + 1) * sub)
        is_ctx = (i * n_sub + u) * sub >= s_lat
        mod = lambda k: jnp.where(is_ctx, modc_ref[k:k + 1, :], modb_ref[k:k + 1, :])
        h_scr[rows, :] = _modulated_rms(x_ref[rows, :], mod(3), mod(4)).astype(BF16)
        for c0 in range(0, FFN_HIDDEN, FFN_TH):
            cols = slice(c0, min(c0 + FFN_TH, FFN_HIDDEN))
            a = jnp.dot(h_scr[rows, :], wa_ref[:, cols], preferred_element_type=F32)
            bb = jnp.dot(h_scr[rows, :], wb_ref[:, cols], preferred_element_type=F32)
            u_scr[rows, cols] = (a * _sigmoid(a) * bb).astype(BF16)
        y = jnp.dot(u_scr[rows, :], wo_ref[...], preferred_element_type=F32)
        o_ref[rows, :] = x_ref[rows, :] + mod(5) * y


def _ffn(xs, mod_l, w_in, w_out, *, s_lat, tm):
    b, t, d = xs.shape
    nb = mod_l.shape[0] - 1
    kern = functools.partial(_ffn_kernel, tm=tm, s_lat=s_lat)
    tok = pl.BlockSpec((None, tm, d), lambda bi, i: (bi, i, 0))
    return pl.pallas_call(
        kern,
        grid=(b, t // tm),
        in_specs=[
            tok,
            pl.BlockSpec((None, N_MOD, d), lambda bi, i: (bi, 0, 0)),
            pl.BlockSpec((None, N_MOD, d), lambda bi, i: (nb, 0, 0)),
            pl.BlockSpec((d, FFN_HIDDEN), lambda bi, i: (0, 0), pipeline_mode=pl.Buffered(1)),
            pl.BlockSpec((d, FFN_HIDDEN), lambda bi, i: (0, 1), pipeline_mode=pl.Buffered(1)),
            _resident((FFN_HIDDEN, d)),
        ],
        out_specs=tok,
        out_shape=jax.ShapeDtypeStruct((b, t, d), F32),
        scratch_shapes=[pltpu.VMEM((tm, d), BF16), pltpu.VMEM((tm, FFN_HIDDEN), BF16)],
        compiler_params=_cparams(("arbitrary", "arbitrary")),
        name="ffn",
    )(xs, mod_l, mod_l, w_in, w_in, w_out)


def _rope_tables(s, lc):
    pos = jnp.arange(s)
    r_ids = (pos // GRID_W).astype(F32)
    c_ids = (pos % GRID_W).astype(F32)
    freqs = ROPE_THETA ** (-jnp.arange(0, AXIS_DIM, 2, dtype=F32) / AXIS_DIM)
    ang_r = r_ids[:, None] * freqs
    ang_c = c_ids[:, None] * freqs
    cos_h = jnp.concatenate([jnp.cos(ang_r), jnp.cos(ang_r), jnp.cos(ang_c), jnp.cos(ang_c)], axis=1)
    sin_h = jnp.concatenate([-jnp.sin(ang_r), jnp.sin(ang_r), -jnp.sin(ang_c), jnp.sin(ang_c)], axis=1)
    reps = LANES // HEAD_DIM
    cos_t = jnp.concatenate([jnp.tile(cos_h, (1, reps)), jnp.ones((lc, LANES), F32)], axis=0)
    sin_t = jnp.concatenate([jnp.tile(sin_h, (1, reps)), jnp.zeros((lc, LANES), F32)], axis=0)
    return cos_t, sin_t


def kernel(x, c, ctx, c_ctx, w_ada, b_ada, w_in, q_norm, k_norm, w_attn_o, conf_dw_w, conf_dw_b,
           conf_ln_g, conf_ln_b, w_conf_out, sc_dw_w, w_sc_out, w_mix_out, w_ffn_in, w_ffn_out):
    b, s, d = x.shape
    lc = ctx.shape[1]
    n_layers = w_ada.shape[0]
    assert d == D_MODEL and lc == ATTN_ROWS and s % 1024 == 0 and (s + lc) % 768 == 0

    xs = jnp.concatenate([x, ctx], axis=1)

    mp = -(-(b + 1) // SUBLANES) * SUBLANES
    cc = jnp.concatenate([c, jnp.zeros((mp - b - 1, d), F32), c_ctx[None, :]], axis=0)
    mod = _modulation(cc, w_ada, b_ada).reshape(n_layers, mp, N_MOD, d)

    cos_t, sin_t = _rope_tables(s, lc)
    head_of_lane = jnp.arange(LANES) // HEAD_DIM
    seg = (head_of_lane[:, None] == head_of_lane[None, :]).astype(BF16) * (1.0 / HEAD_DIM)
    w_in, w_attn_o, w_conf_out, w_sc_out, w_mix_out, w_ffn_in, w_ffn_out = (
        w.astype(BF16) for w in (w_in, w_attn_o, w_conf_out, w_sc_out, w_mix_out, w_ffn_in, w_ffn_out))

    for i in range(n_layers):
        gvec = jnp.concatenate([jnp.tile(q_norm[i] * (ATTN_SCALE * LOG2E), N_Q_HEADS),
                                jnp.tile(k_norm[i], N_KV_HEADS)])[None, :]
        last = i == n_layers - 1
        qkv, mid, gates = _in_proj(xs, mod[i], w_in[i], gvec, cos_t, sin_t, seg, s_lat=s, tm=768)
        conv_w = (conf_dw_w[i], conf_dw_b[i][None, :], conf_ln_g[i][None, :], conf_ln_b[i][None, :])
        outs = _attention(qkv, mid, None, *conv_w, q_row0=0, n_q=s, tq=1024, k_row0=0, n_k=s + lc)
        if not last:
            outs = _attention(qkv, mid, outs, *conv_w, q_row0=s, n_q=lc, tq=lc, k_row0=s, n_k=lc)
        attn, hn = outs
        xs = _merge(xs, mid, gates, attn, hn, mod[i], w_attn_o[i], w_conf_out[i], w_sc_out[i], w_mix_out[i],
                    sc_dw_w[i], s_lat=s, tm=512 if last else 768, out_rows=s if last else s + lc)
        xs = _ffn(xs, mod[i], w_ffn_in[i], w_ffn_out[i], s_lat=s, tm=512 if last else 768)
    return xs
```
